```python
import math
import jax, jax.numpy as jnp
from jax import lax
import numpy as np

D_MODEL = 2048
BATCH = 1
SEQ = 16384
DEPTH = 2

N_A_LAYERS = DEPTH // 2
N_B_LAYERS = DEPTH - N_A_LAYERS
EPS = 1e-6

GLA_HEADS = 4
GLA_KEY_DIM = D_MODEL // 2
GLA_VAL_DIM = D_MODEL
GLA_HEAD_K = GLA_KEY_DIM // GLA_HEADS
GLA_HEAD_V = GLA_VAL_DIM // GLA_HEADS
GLA_GATE_RANK = 16
GLA_GATE_NORMALIZER = 16.0
GLA_CHUNK = 64

MOBA_HEADS = 16
MOBA_HEAD_DIM = D_MODEL // MOBA_HEADS
MOBA_BLOCK = 256
MOBA_TOPK = 3
MOBA_Q_CHUNK = 32

REL_BUCKETS = 32
REL_MAX_DISTANCE = 128

FFN_HIDDEN = ((8 * D_MODEL // 3 + 255) // 256) * 256
CONV_WIDTH = 3

kernel_name = "yoco_gla_moba_convffn_t5bias"


def rms_norm(x, w, eps=EPS):
    xf = x.astype(jnp.float32)
    y = xf * lax.rsqrt(jnp.mean(xf * xf, axis=-1, keepdims=True) + eps)
    return (y * w.astype(jnp.float32)).astype(x.dtype)


def rel_bucket(dist):
    max_exact = REL_BUCKETS // 2
    n = jnp.maximum(dist, 0)
    large = max_exact + (jnp.log(jnp.maximum(n, 1).astype(jnp.float32) / max_exact)
                         / math.log(REL_MAX_DISTANCE / max_exact)
                         * (REL_BUCKETS - max_exact)).astype(jnp.int32)
    large = jnp.minimum(large, REL_BUCKETS - 1)
    return jnp.where(n < max_exact, n, large)


def gla_mixer(x, norm_w, w_in, gk_w1, gk_w2, gk_b, o_norm_w, w_out):
    B, S, _ = x.shape
    H, dk, dv, C = GLA_HEADS, GLA_HEAD_K, GLA_HEAD_V, GLA_CHUNK
    nc = S // C
    f32 = jnp.float32
    xn = rms_norm(x, norm_w)
    proj = xn @ w_in
    q, k, v, g = jnp.split(proj, [GLA_KEY_DIM, 2 * GLA_KEY_DIM, 2 * GLA_KEY_DIM + GLA_VAL_DIM], axis=-1)
    gk = (xn @ gk_w1) @ gk_w2 + gk_b
    log_a = jax.nn.log_sigmoid(gk.astype(f32)) / GLA_GATE_NORMALIZER

    def to_chunks(t, d):
        return t.astype(f32).reshape(B, nc, C, H, d).transpose(0, 3, 1, 2, 4)

    q = to_chunks(q, dk) * (dk ** -0.5)
    k = to_chunks(k, dk)
    v = to_chunks(v, dv)
    b = jnp.cumsum(to_chunks(log_a, dk), axis=3)
    b_last = b[:, :, :, -1:, :]
    q_dec = q * jnp.exp(b)
    k_inv = k * jnp.exp(-b)
    k_end = k * jnp.exp(b_last - b)
    causal = jnp.tril(jnp.ones((C, C), dtype=bool))
    attn = jnp.where(causal, jnp.einsum('bhnqk,bhnsk->bhnqs', q_dec, k_inv), 0.0)
    o_intra = jnp.einsum('bhnqs,bhnsv->bhnqv', attn, v)
    chunk_decay = jnp.exp(b_last[:, :, :, 0, :])

    def step(state, inp):
        q_c, k_c, v_c, d_c = inp
        o_c = jnp.einsum('bhqk,bhkv->bhqv', q_c, state)
        state = state * d_c[..., None] + jnp.einsum('bhsk,bhsv->bhkv', k_c, v_c)
        return state, o_c

    xs = (jnp.moveaxis(q_dec, 2, 0), jnp.moveaxis(k_end, 2, 0),
          jnp.moveaxis(v, 2, 0), jnp.moveaxis(chunk_decay, 2, 0))
    _, o_inter = lax.scan(step, jnp.zeros((B, H, dk, dv), f32), xs)
    o = o_intra + jnp.moveaxis(o_inter, 0, 2)
    o = o.transpose(0, 2, 3, 1, 4).reshape(B, S, H, dv)
    o = rms_norm(o, o_norm_w) * jax.nn.silu(g.astype(f32)).reshape(B, S, H, dv)
    return o.reshape(B, S, GLA_VAL_DIM).astype(x.dtype) @ w_out


def conv_ffn(x, norm_w, w_up, conv_w, conv_b, w_down):
    S = x.shape[1]
    h = rms_norm(x, norm_w) @ w_up
    hp = jnp.pad(h, ((0, 0), (CONV_WIDTH - 1, 0), (0, 0)))
    acc = conv_b
    for j in range(CONV_WIDTH):
        acc = acc + conv_w[j] * hp[:, j:j + S]
    a, u = jnp.split(acc, 2, axis=-1)
    return (jax.nn.silu(a) * u) @ w_down


def shared_kv(x, norm_w, w_kv, k_norm_w):
    B, S, _ = x.shape
    H, Dh, BLK = MOBA_HEADS, MOBA_HEAD_DIM, MOBA_BLOCK
    nb = -(-S // BLK)
    s_pad = nb * BLK
    kv = rms_norm(x, norm_w) @ w_kv
    k, v = jnp.split(kv, 2, axis=-1)
    k = rms_norm(k.reshape(B, S, H, Dh), k_norm_w)
    v = v.reshape(B, S, H, Dh)
    pad = ((0, 0), (0, s_pad - S), (0, 0), (0, 0))
    k_blk = jnp.pad(k, pad).reshape(B, nb, BLK, H, Dh).transpose(0, 3, 1, 2, 4)
    v_blk = jnp.pad(v, pad).reshape(B, nb, BLK, H, Dh).transpose(0, 3, 1, 2, 4)
    k_mean = jnp.mean(k_blk.astype(jnp.float32), axis=3).astype(k_blk.dtype)
    return k_blk, v_blk, k_mean


def moba_mixer(x, norm_w, w_q, q_norm_w, w_out, k_blk, v_blk, k_mean, rel_bias):
    B, S, _ = x.shape
    H, Dh, BLK, QC = MOBA_HEADS, MOBA_HEAD_DIM, MOBA_BLOCK, MOBA_Q_CHUNK
    nb = k_blk.shape[2]
    topk = min(MOBA_TOPK, nb)
    q = (rms_norm(x, norm_w) @ w_q).reshape(B, S, H, Dh)
    q = (rms_norm(q, q_norm_w) * (Dh ** -0.5)).transpose(0, 2, 1, 3)
    bias_table = rel_bias.T
    b_idx = jnp.arange(B)[:, None, None, None]
    h_idx = jnp.arange(H)[None, :, None, None]
    blk_ids = jnp.arange(nb)
    offs = jnp.arange(BLK)

    def attend(c):
        start = c * QC
        q_c = lax.dynamic_slice_in_dim(q, start, QC, axis=2)
        t = start + jnp.arange(QC)
        own = start // BLK
        gate = jnp.einsum('bhqd,bhnd->bhqn', q_c, k_mean).astype(jnp.float32)
        gate = jnp.where(blk_ids < own, gate, -jnp.inf)
        _, sel = lax.top_k(gate, topk)
        sel_ok = jnp.arange(topk) < own
        k_sel = k_blk[b_idx, h_idx, sel]
        v_sel = v_blk[b_idx, h_idx, sel]
        pos_sel = sel[..., None] * BLK + offs
        s_sel = jnp.einsum('bhqd,bhqjkd->bhqjk', q_c, k_sel).astype(jnp.float32)
        s_sel = s_sel + bias_table[h_idx[..., None], rel_bucket(t[:, None, None] - pos_sel)]
        s_sel = jnp.where(sel_ok[:, None], s_sel, -jnp.inf)
        k_own = lax.dynamic_index_in_dim(k_blk, own, axis=2, keepdims=False)
        v_own = lax.dynamic_index_in_dim(v_blk, own, axis=2, keepdims=False)
        dist_own = t[:, None] - (own * BLK + offs)[None, :]
        s_own = jnp.einsum('bhqd,bhkd->bhqk', q_c, k_own).astype(jnp.float32)
        s_own = s_own + bias_table[:, rel_bucket(dist_own)][None]
        s_own = jnp.where(dist_own >= 0, s_own, -jnp.inf)
        scores = jnp.concatenate([s_sel.reshape(B, H, QC, topk * BLK), s_own], axis=-1)
        p = jax.nn.softmax(scores, axis=-1).astype(v_blk.dtype)
        p_sel = p[..., :topk * BLK].reshape(B, H, QC, topk, BLK)
        p_own = p[..., topk * BLK:]
        return (jnp.einsum('bhqjk,bhqjkd->bhqd', p_sel, v_sel)
                + jnp.einsum('bhqk,bhkd->bhqd', p_own, v_own))

    o = lax.map(attend, jnp.arange(S // QC))
    o = o.transpose(1, 0, 3, 2, 4).reshape(B, S, H * Dh)
    return o @ w_out


def setup_inputs(seed: int = 0) -> dict:
    key = jax.random.key(seed)
    ks = jax.random.split(key, 21)
    f32 = jnp.float32

    def nrm(k, shape, scale):
        return jax.random.normal(k, shape, f32) * scale

    def gain(k, shape):
        return 1.0 + 0.02 * jax.random.normal(k, shape, f32)

    D, KD, VD, R = D_MODEL, GLA_KEY_DIM, GLA_VAL_DIM, GLA_GATE_RANK
    HD = MOBA_HEADS * MOBA_HEAD_DIM
    F = FFN_HIDDEN
    return {
        "x": nrm(ks[0], (BATCH, SEQ, D), 1.0),
        "gla_norm": gain(ks[1], (N_A_LAYERS, D)),
        "gla_w_in": nrm(ks[2], (N_A_LAYERS, D, 2 * KD + 2 * VD), D ** -0.5),
        "gla_gk_w1": nrm(ks[3], (N_A_LAYERS, D, R), D ** -0.5),
        "gla_gk_w2": nrm(ks[4], (N_A_LAYERS, R, KD), R ** -0.5),
        "gla_gk_b": nrm(ks[5], (N_A_LAYERS, KD), 0.1),
        "gla_o_norm": gain(ks[6], (N_A_LAYERS, GLA_HEAD_V)),
        "gla_w_out": nrm(ks[7], (N_A_LAYERS, VD, D), VD ** -0.5),
        "kv_norm": gain(ks[8], (D,)),
        "kv_w": nrm(ks[9], (D, 2 * HD), D ** -0.5),
        "k_norm_w": gain(ks[10], (MOBA_HEAD_DIM,)),
        "moba_norm": gain(ks[11], (N_B_LAYERS, D)),
        "moba_w_q": nrm(ks[12], (N_B_LAYERS, D, HD), D ** -0.5),
        "moba_q_norm": gain(ks[13], (N_B_LAYERS, MOBA_HEAD_DIM)),
        "moba_w_out": nrm(ks[14], (N_B_LAYERS, HD, D), HD ** -0.5),
        "rel_bias": nrm(ks[15], (REL_BUCKETS, MOBA_HEADS), 0.2),
        "ffn_norm": gain(ks[16], (DEPTH, D)),
        "ffn_w_up": nrm(ks[17], (DEPTH, D, 2 * F), D ** -0.5),
        "ffn_conv_w": nrm(ks[18], (DEPTH, CONV_WIDTH, 2 * F), CONV_WIDTH ** -0.5),
        "ffn_conv_b": nrm(ks[19], (DEPTH, 2 * F), 0.02),
        "ffn_w_down": nrm(ks[20], (DEPTH, F, D), F ** -0.5),
    }


def reference(x, gla_norm, gla_w_in, gla_gk_w1, gla_gk_w2, gla_gk_b, gla_o_norm, gla_w_out,
              kv_norm, kv_w, k_norm_w, moba_norm, moba_w_q, moba_q_norm, moba_w_out, rel_bias,
              ffn_norm, ffn_w_up, ffn_conv_w, ffn_conv_b, ffn_w_down):
    h = x
    k_blk = v_blk = k_mean = None
    for layer in range(DEPTH):
        if layer < N_A_LAYERS:
            i = layer
            h = h + gla_mixer(h, gla_norm[i], gla_w_in[i], gla_gk_w1[i], gla_gk_w2[i],
                              gla_gk_b[i], gla_o_norm[i], gla_w_out[i])
        else:
            if layer == N_A_LAYERS:
                k_blk, v_blk, k_mean = shared_kv(h, kv_norm, kv_w, k_norm_w)
            j = layer - N_A_LAYERS
            h = h + moba_mixer(h, moba_norm[j], moba_w_q[j], moba_q_norm[j], moba_w_out[j],
                               k_blk, v_blk, k_mean, rel_bias)
        h = h + conv_ffn(h, ffn_norm[layer], ffn_w_up[layer], ffn_conv_w[layer],
                         ffn_conv_b[layer], ffn_w_down[layer])
    return h
```

```python
import functools
import math

import jax
import jax.numpy as jnp
from jax import lax
from jax.experimental import pallas as pl
from jax.experimental.pallas import tpu as pltpu

F32 = jnp.float32
BF16 = jnp.bfloat16

EPS = 1e-6
GLA_HEADS = 4
GLA_CHUNK = 64
GLA_GATE_NORMALIZER = 16.0
MOBA_HEADS = 16
MOBA_HEAD_DIM = 128
MOBA_BLOCK = 256
MOBA_TOPK = 3
REL_BUCKETS = 32
REL_MAX_DISTANCE = 128
CONV_WIDTH = 3

VMEM_LIMIT_BYTES = 56 * 1024 * 1024
LANE = 128
HALO_ROWS = 8


def _cparams(*sem):
    return pltpu.CompilerParams(dimension_semantics=sem,
                                vmem_limit_bytes=VMEM_LIMIT_BYTES)


def _rmsnorm_rows(x, w):
    ms = jnp.mean(x * x, axis=-1, keepdims=True)
    return x * lax.rsqrt(ms + EPS) * w


def _dot(a, b):
    return jnp.dot(a, b, preferred_element_type=F32)


def _dot_nt(a, b):
    return lax.dot_general(a, b, (((1,), (1,)), ((), ())), preferred_element_type=F32)


def _dot_tn(a, b):
    return lax.dot_general(a, b, (((0,), (0,)), ((), ())), preferred_element_type=F32)


def _gla_in_kernel(x_ref, nw_ref, w_ref, w1_ref, w2_ref, b_ref, proj_ref, la_ref, xn_scr):
    @pl.when(pl.program_id(1) == 0)
    def _():
        xn = _rmsnorm_rows(x_ref[...], nw_ref[...]).astype(BF16)
        xn_scr[...] = xn
        r = _dot(xn, w1_ref[...])
        gk = _dot(r.astype(BF16), w2_ref[...]) + b_ref[...]
        log_sig = jnp.minimum(gk, 0.0) - jnp.log1p(jnp.exp(-jnp.abs(gk)))
        la_ref[...] = log_sig * (1.0 / GLA_GATE_NORMALIZER)

    proj_ref[...] = _dot(xn_scr[...], w_ref[...])


def _gla_in(x, norm_w, w_in, w1p, w2p, gk_b, *, tm=1024, tn=512):
    S, D = x.shape
    N = w_in.shape[1]
    KD = w2p.shape[1]
    return pl.pallas_call(
        _gla_in_kernel,
        grid=(S // tm, N // tn),
        in_specs=[
            pl.BlockSpec((tm, D), lambda i, j: (i, 0)),
            pl.BlockSpec((1, D), lambda i, j: (0, 0)),
            pl.BlockSpec((D, tn), lambda i, j: (0, j)),
            pl.BlockSpec((D, LANE), lambda i, j: (0, 0)),
            pl.BlockSpec((LANE, KD), lambda i, j: (0, 0)),
            pl.BlockSpec((1, KD), lambda i, j: (0, 0)),
        ],
        out_specs=[
            pl.BlockSpec((tm, tn), lambda i, j: (i, j)),
            pl.BlockSpec((tm, KD), lambda i, j: (i, 0)),
        ],
        out_shape=[jax.ShapeDtypeStruct((S, N), F32),
                   jax.ShapeDtypeStruct((S, KD), F32)],
        scratch_shapes=[pltpu.VMEM((tm, D), BF16)],
        compiler_params=_cparams("arbitrary", "arbitrary"),
        name="gla_in",
    )(x, norm_w, w_in, w1p, w2p, gk_b)


def _gla_core_kernel(q_ref, k_ref, v_ref, g_ref, la_ref, onw_ref, o_ref, st_ref, *, n_chunks, dk):
    C = GLA_CHUNK

    @pl.when(pl.program_id(1) == 0)
    def _():
        st_ref[...] = jnp.zeros_like(st_ref)

    row = lax.broadcasted_iota(jnp.int32, (C, C), 0)
    col = lax.broadcasted_iota(jnp.int32, (C, C), 1)
    causal = row >= col
    tril = causal.astype(BF16)
    q_scale = dk ** -0.5

    def chunk(c, carry):
        r0 = pl.multiple_of(c * C, C)
        rows = pl.ds(r0, C)
        la = la_ref[rows, :]
        la_hi = la.astype(BF16)
        la_lo = (la - la_hi.astype(F32)).astype(BF16)
        b = _dot(tril, la_hi) + _dot(tril, la_lo)
        b_last = b[C - 1:C, :]
        q = q_ref[rows, :]
        k = k_ref[rows, :]
        v = v_ref[rows, :].astype(BF16)
        q_dec = ((q * q_scale) * jnp.exp(b)).astype(BF16)
        k_inv = (k * jnp.exp(-b)).astype(BF16)
        k_end = (k * jnp.exp(b_last - b)).astype(BF16)
        chunk_decay = jnp.exp(b_last)
        attn = jnp.where(causal, _dot_nt(q_dec, k_inv), 0.0)
        st = st_ref[...]
        o = _dot(attn.astype(BF16), v) + _dot_nt(q_dec, st.astype(BF16))
        st_ref[...] = st * chunk_decay + _dot_tn(v, k_end)
        y = _rmsnorm_rows(o, onw_ref[...])
        g = g_ref[rows, :]
        y = y * (g * (1.0 / (1.0 + jnp.exp(-g))))
        o_ref[rows, :] = y.astype(BF16)
        return carry

    lax.fori_loop(0, n_chunks, chunk, 0)


def _gla_core(proj, log_a, o_norm_w, *, tile=512):
    S = proj.shape[0]
    H = GLA_HEADS
    KD = log_a.shape[1]
    dk = KD // H
    VD = (proj.shape[1] - 2 * KD) // 2
    dv = VD // H
    kq, kv_ = KD // dk, VD // dv
    return pl.pallas_call(
        functools.partial(_gla_core_kernel, n_chunks=tile // GLA_CHUNK, dk=dk),
        grid=(H, S // tile),
        in_specs=[
            pl.BlockSpec((tile, dk), lambda h, t: (t, h)),
            pl.BlockSpec((tile, dk), lambda h, t: (t, kq + h)),
            pl.BlockSpec((tile, dv), lambda h, t: (t, (2 * KD) // dv + h)),
            pl.BlockSpec((tile, dv), lambda h, t: (t, (2 * KD + VD) // dv + h)),
            pl.BlockSpec((tile, dk), lambda h, t: (t, h)),
            pl.BlockSpec((1, dv), lambda h, t: (0, 0)),
        ],
        out_specs=pl.BlockSpec((tile, dv), lambda h, t: (t, h)),
        out_shape=jax.ShapeDtypeStruct((S, VD), BF16),
        scratch_shapes=[pltpu.VMEM((dv, dk), F32)],
        compiler_params=_cparams("arbitrary", "arbitrary"),
        name="gla_core",
    )(proj, proj, proj, proj, log_a, o_norm_w)


def _mm_res_kernel(a_ref, w_ref, r_ref, o_ref):
    o_ref[...] = r_ref[...] + _dot(a_ref[...], w_ref[...])


def _mm_res(a, w, res, *, tm=1024, tn=1024):
    S, K = a.shape
    N = w.shape[1]
    return pl.pallas_call(
        _mm_res_kernel,
        grid=(S // tm, N // tn),
        in_specs=[
            pl.BlockSpec((tm, K), lambda i, j: (i, 0)),
            pl.BlockSpec((K, tn), lambda i, j: (0, j)),
            pl.BlockSpec((tm, tn), lambda i, j: (i, j)),
        ],
        out_specs=pl.BlockSpec((tm, tn), lambda i, j: (i, j)),
        out_shape=jax.ShapeDtypeStruct((S, N), F32),
        compiler_params=_cparams("arbitrary", "arbitrary"),
        name="mm_res",
    )(a, w, res)


def _ffn_kernel(x_ref, nw_ref, wa_ref, wu_ref, cwa_ref, cwu_ref, cba_ref, cbu_ref, wd_ref,
                o_ref, xn_scr, halo_a, halo_u):
    i = pl.program_id(0)
    j = pl.program_id(1)

    @pl.when(j == 0)
    def _():
        x = x_ref[...]
        xn_scr[...] = _rmsnorm_rows(x, nw_ref[...]).astype(BF16)
        o_ref[...] = x

    @pl.when(i == 0)
    def _():
        halo_a[j] = jnp.zeros(halo_a.shape[1:], F32)
        halo_u[j] = jnp.zeros(halo_u.shape[1:], F32)

    xn = xn_scr[...]
    tm = xn.shape[0]
    tf = wa_ref.shape[1]
    row = lax.broadcasted_iota(jnp.int32, (tm, tf), 0)
    is_row0 = row == 0
    is_row1 = row == 1

    def conv(h, halo_ref, cw_ref, cb_ref):
        halo = halo_ref[j]
        prev1 = halo[HALO_ROWS - 1:HALO_ROWS, :]
        prev2 = halo[HALO_ROWS - 2:HALO_ROWS - 1, :]
        h1 = jnp.where(is_row0, prev1, pltpu.roll(h, 1, 0))
        h2 = jnp.where(is_row0, prev2, jnp.where(is_row1, prev1, pltpu.roll(h, 2, 0)))
        halo_ref[j] = h[tm - HALO_ROWS:tm, :]
        return cb_ref[...] + cw_ref[0:1, :] * h2 + cw_ref[1:2, :] * h1 + cw_ref[2:3, :] * h

    ca = conv(_dot(xn, wa_ref[...]), halo_a, cwa_ref, cba_ref)
    cu = conv(_dot(xn, wu_ref[...]), halo_u, cwu_ref, cbu_ref)
    act = (ca * (1.0 / (1.0 + jnp.exp(-ca)))) * cu
    o_ref[...] += _dot(act.astype(BF16), wd_ref[...])


def _conv_ffn(x, norm_w, w_up, conv_w, conv_b, w_down, *, tm=512, tf=512):
    S, D = x.shape
    Fh = w_down.shape[0]
    nF = Fh // tf
    return pl.pallas_call(
        _ffn_kernel,
        grid=(S // tm, nF),
        in_specs=[
            pl.BlockSpec((tm, D), lambda i, j: (i, 0)),
            pl.BlockSpec((1, D), lambda i, j: (0, 0)),
            pl.BlockSpec((D, tf), lambda i, j: (0, j)),
            pl.BlockSpec((D, tf), lambda i, j: (0, nF + j)),
            pl.BlockSpec((CONV_WIDTH, tf), lambda i, j: (0, j)),
            pl.BlockSpec((CONV_WIDTH, tf), lambda i, j: (0, nF + j)),
            pl.BlockSpec((1, tf), lambda i, j: (0, j)),
            pl.BlockSpec((1, tf), lambda i, j: (0, nF + j)),
            pl.BlockSpec((tf, D), lambda i, j: (j, 0)),
        ],
        out_specs=pl.BlockSpec((tm, D), lambda i, j: (i, 0)),
        out_shape=jax.ShapeDtypeStruct((S, D), F32),
        scratch_shapes=[pltpu.VMEM((tm, D), BF16),
                        pltpu.VMEM((nF, HALO_ROWS, tf), F32),
                        pltpu.VMEM((nF, HALO_ROWS, tf), F32)],
        compiler_params=_cparams("arbitrary", "arbitrary"),
        name="conv_ffn",
    )(x, norm_w, w_up, w_up, conv_w, conv_w, conv_b, conv_b, w_down)


def _proj_headnorm_kernel(x_ref, nw_ref, w_ref, hw_ref, o_ref, xn_scr, *, scale):
    @pl.when(pl.program_id(1) == 0)
    def _():
        xn_scr[...] = _rmsnorm_rows(x_ref[...], nw_ref[...]).astype(BF16)

    y = _dot(xn_scr[...], w_ref[...])
    Dh = hw_ref.shape[1]
    for hh in range(y.shape[1] // Dh):
        cols = slice(hh * Dh, (hh + 1) * Dh)
        yn = _rmsnorm_rows(y[:, cols], hw_ref[...])
        if scale is not None:
            yn = yn * scale
        o_ref[:, cols] = yn.astype(BF16)


def _proj_headnorm(x, norm_w, w, head_w, n_out, *, scale=None, tm=1024, tn=512):
    S, D = x.shape
    Dh = head_w.shape[1]
    return pl.pallas_call(
        functools.partial(_proj_headnorm_kernel, scale=scale),
        grid=(S // tm, n_out // tn),
        in_specs=[
            pl.BlockSpec((tm, D), lambda i, j: (i, 0)),
            pl.BlockSpec((1, D), lambda i, j: (0, 0)),
            pl.BlockSpec((D, tn), lambda i, j: (0, j)),
            pl.BlockSpec((1, Dh), lambda i, j: (0, 0)),
        ],
        out_specs=pl.BlockSpec((tm, tn), lambda i, j: (i, j)),
        out_shape=jax.ShapeDtypeStruct((S, n_out), BF16),
        scratch_shapes=[pltpu.VMEM((tm, D), BF16)],
        compiler_params=_cparams("arbitrary", "arbitrary"),
        name="proj_headnorm",
    )(x, norm_w, w, head_w)


def _vproj_kernel(x_ref, nw_ref, w_ref, vt_ref, xn_scr):
    @pl.when(pl.program_id(1) == 0)
    def _():
        xn_scr[...] = _rmsnorm_rows(x_ref[...], nw_ref[...]).astype(BF16)

    y = _dot(xn_scr[...], w_ref[...])
    n_h, n_b, Dh, BLK = vt_ref.shape
    for hh in range(n_h):
        for bb in range(n_b):
            blk = y[bb * BLK:(bb + 1) * BLK, hh * Dh:(hh + 1) * Dh]
            vt_ref[hh, bb] = blk.T.astype(BF16)


def _vproj(x, norm_w, w_kv, col0, *, tm=1024, tn=512):
    S, D = x.shape
    H, Dh, BLK = MOBA_HEADS, MOBA_HEAD_DIM, MOBA_BLOCK
    off = col0 // tn
    return pl.pallas_call(
        _vproj_kernel,
        grid=(S // tm, (H * Dh) // tn),
        in_specs=[
            pl.BlockSpec((tm, D), lambda i, j: (i, 0)),
            pl.BlockSpec((1, D), lambda i, j: (0, 0)),
            pl.BlockSpec((D, tn), lambda i, j: (0, off + j)),
        ],
        out_specs=pl.BlockSpec((tn // Dh, tm // BLK, Dh, BLK), lambda i, j: (j, i, 0, 0)),
        out_shape=jax.ShapeDtypeStruct((H, S // BLK, Dh, BLK), BF16),
        scratch_shapes=[pltpu.VMEM((tm, D), BF16)],
        compiler_params=_cparams("arbitrary", "arbitrary"),
        name="v_proj",
    )(x, norm_w, w_kv)


def _kmean_kernel(k_ref, o_ref):
    rows, cols = k_ref.shape
    k = k_ref[...].astype(F32).reshape(rows // MOBA_BLOCK, MOBA_BLOCK, cols)
    o_ref[...] = jnp.sum(k, axis=1) * (1.0 / MOBA_BLOCK)


def _kmean(k, *, tr=2048, tc=512):
    S, N = k.shape
    return pl.pallas_call(
        _kmean_kernel,
        grid=(S // tr, N // tc),
        in_specs=[pl.BlockSpec((tr, tc), lambda i, j: (i, j))],
        out_specs=pl.BlockSpec((tr // MOBA_BLOCK, tc), lambda i, j: (i, j)),
        out_shape=jax.ShapeDtypeStruct((S // MOBA_BLOCK, N), F32),
        compiler_params=_cparams("arbitrary", "arbitrary"),
        name="kmean",
    )(k)


def _gate_kernel(q_ref, km_ref, mask_ref):
    nb = km_ref.shape[0]
    tq = q_ref.shape[0]
    g = _dot_nt(km_ref[...].astype(BF16), q_ref[...])
    blk = lax.broadcasted_iota(jnp.int32, (nb, tq), 0)
    pos = pl.program_id(1) * tq + lax.broadcasted_iota(jnp.int32, (nb, tq), 1)
    own = lax.shift_right_logical(pos, int(math.log2(MOBA_BLOCK)))
    neg = -jnp.inf
    v = jnp.where(blk < own, g, neg)
    sel = jnp.zeros((nb, tq), jnp.bool_)
    for r in range(min(MOBA_TOPK, nb)):
        m = jnp.max(v, axis=0, keepdims=True)
        first = jnp.min(jnp.where(v == m, blk, nb), axis=0, keepdims=True)
        onehot = blk == first
        sel = sel | (onehot & (own > r))
        v = jnp.where(onehot, neg, v)
    mask_ref[0] = jnp.where(sel, 0.0, neg)


def _gate_mask(q, kmean, *, tq=2048):
    S = q.shape[0]
    H, Dh = MOBA_HEADS, MOBA_HEAD_DIM
    nb = kmean.shape[0]
    return pl.pallas_call(
        _gate_kernel,
        grid=(H, S // tq),
        in_specs=[
            pl.BlockSpec((tq, Dh), lambda h, t: (t, h)),
            pl.BlockSpec((nb, Dh), lambda h, t: (0, h)),
        ],
        out_specs=pl.BlockSpec((1, nb, tq), lambda h, t: (h, 0, t)),
        out_shape=jax.ShapeDtypeStruct((H, nb, S), F32),
        compiler_params=_cparams("arbitrary", "arbitrary"),
        name="gate_topk",
    )(q, kmean)


def _rel_bias_tile(tbl_ref, h, dist):
    max_exact = REL_BUCKETS // 2
    n = jnp.maximum(dist, 0)
    large = max_exact + (jnp.log(jnp.maximum(n, 1).astype(F32) / max_exact)
                         / math.log(REL_MAX_DISTANCE / max_exact)
                         * (REL_BUCKETS - max_exact)).astype(jnp.int32)
    large = jnp.minimum(large, REL_BUCKETS - 1)
    bucket = jnp.where(n < max_exact, n, large)
    out = jnp.zeros(dist.shape, F32)
    for b in range(REL_BUCKETS):
        out = jnp.where(bucket == b, tbl_ref[h, b], out)
    return out


def _moba_kernel(tbl_ref, q_ref, k_ref, vt_ref, mask_ref, o_ref, bown_ref, bprev_ref):
    BLK = MOBA_BLOCK
    h = pl.program_id(0)
    i = pl.program_id(1)

    @pl.when(i == 0)
    def _():
        key = lax.broadcasted_iota(jnp.int32, (BLK, BLK), 0)
        qry = lax.broadcasted_iota(jnp.int32, (BLK, BLK), 1)
        dist = qry - key
        bown_ref[...] = jnp.where(dist >= 0, _rel_bias_tile(tbl_ref, h, dist), -jnp.inf)
        bprev_ref[...] = _rel_bias_tile(tbl_ref, h, dist + BLK)

    qi = q_ref[...]

    def scores(j):
        kj = k_ref[pl.ds(pl.multiple_of(j * BLK, BLK), BLK), :]
        return _dot_nt(kj, qi)

    s = scores(i) + bown_ref[...]
    m = jnp.max(s, axis=0, keepdims=True)
    p = jnp.exp(s - m)
    l = jnp.sum(p, axis=0, keepdims=True)
    acc = _dot(vt_ref[0, i], p.astype(BF16))

    def step(j, bias, carry):
        m, l, acc = carry
        s = scores(j) + bias + mask_ref[0, pl.ds(j, 1), :]
        m_new = jnp.maximum(m, jnp.max(s, axis=0, keepdims=True))
        alpha = jnp.exp(m - m_new)
        p = jnp.exp(s - m_new)
        l = alpha * l + jnp.sum(p, axis=0, keepdims=True)
        acc = alpha * acc + _dot(vt_ref[0, j], p.astype(BF16))
        return m_new, l, acc

    carry = step(jnp.maximum(i - 1, 0), bprev_ref[...], (m, l, acc))
    far_bias = tbl_ref[h, REL_BUCKETS - 1]
    m, l, acc = lax.fori_loop(0, i - 1, lambda j, c: step(j, far_bias, c), carry)
    o_ref[...] = (acc / l).T.astype(BF16)


def _moba_attn(tbl, q, k, vt, mask):
    S = q.shape[0]
    H, Dh, BLK = MOBA_HEADS, MOBA_HEAD_DIM, MOBA_BLOCK
    nb = S // BLK
    return pl.pallas_call(
        _moba_kernel,
        grid=(H, nb),
        in_specs=[
            pl.BlockSpec(memory_space=pltpu.SMEM),
            pl.BlockSpec((BLK, Dh), lambda h, i: (i, h)),
            pl.BlockSpec((S, Dh), lambda h, i: (0, h)),
            pl.BlockSpec((1, nb, Dh, BLK), lambda h, i: (h, 0, 0, 0)),
            pl.BlockSpec((1, nb, BLK), lambda h, i: (h, 0, i)),
        ],
        out_specs=pl.BlockSpec((BLK, Dh), lambda h, i: (i, h)),
        out_shape=jax.ShapeDtypeStruct((S, H * Dh), BF16),
        scratch_shapes=[pltpu.VMEM((BLK, BLK), F32), pltpu.VMEM((BLK, BLK), F32)],
        compiler_params=_cparams("arbitrary", "arbitrary"),
        name="moba_attn",
    )(tbl, q, k, vt, mask)


def _row(v):
    return v.reshape(1, -1)


def kernel(x, gla_norm, gla_w_in, gla_gk_w1, gla_gk_w2, gla_gk_b, gla_o_norm, gla_w_out,
           kv_norm, kv_w, k_norm_w, moba_norm, moba_w_q, moba_q_norm, moba_w_out, rel_bias,
           ffn_norm, ffn_w_up, ffn_conv_w, ffn_conv_b, ffn_w_down):
    B, S, D = x.shape
    assert B == 1
    depth = ffn_norm.shape[0]
    n_a = gla_norm.shape[0]
    h = x[0]
    HD = MOBA_HEADS * MOBA_HEAD_DIM
    k_bf = vt = kmean = None
    tbl = rel_bias.T

    for layer in range(depth):
        if layer < n_a:
            a = layer
            R = gla_gk_w1.shape[2]
            w1p = jnp.pad(gla_gk_w1[a].astype(BF16), ((0, 0), (0, LANE - R)))
            w2p = jnp.pad(gla_gk_w2[a].astype(BF16), ((0, LANE - R), (0, 0)))
            proj, log_a = _gla_in(h, _row(gla_norm[a]), gla_w_in[a].astype(BF16), w1p, w2p,
                                  _row(gla_gk_b[a]))
            o = _gla_core(proj, log_a, _row(gla_o_norm[a]))
            h = _mm_res(o, gla_w_out[a].astype(BF16), h)
        else:
            b = layer - n_a
            if k_bf is None:
                kv_bf = kv_w.astype(BF16)
                k_bf = _proj_headnorm(h, _row(kv_norm), kv_bf, _row(k_norm_w), HD)
                vt = _vproj(h, _row(kv_norm), kv_bf, HD)
                kmean = _kmean(k_bf)
            q = _proj_headnorm(h, _row(moba_norm[b]), moba_w_q[b].astype(BF16),
                               _row(moba_q_norm[b]), HD, scale=MOBA_HEAD_DIM ** -0.5)
            mask = _gate_mask(q, kmean)
            o = _moba_attn(tbl, q, k_bf, vt, mask)
            h = _mm_res(o, moba_w_out[b].astype(BF16), h)
        h = _conv_ffn(h, _row(ffn_norm[layer]), ffn_w_up[layer].astype(BF16),
                      ffn_conv_w[layer], _row(ffn_conv_b[layer]), ffn_w_down[layer].astype(BF16))
    return h[None]
```

```python
import functools
import math

import jax
import jax.numpy as jnp
from jax import lax
from jax.experimental import pallas as pl
from jax.experimental.pallas import tpu as pltpu

F32 = jnp.float32
BF16 = jnp.bfloat16

EPS = 1e-6
GLA_HEADS = 4
GLA_CHUNK = 64
GLA_GATE_NORMALIZER = 16.0
MOBA_HEADS = 16
MOBA_HEAD_DIM = 128
MOBA_BLOCK = 256
MOBA_TOPK = 3
REL_BUCKETS = 32
REL_MAX_DISTANCE = 128
CONV_WIDTH = 3

VMEM_LIMIT_BYTES = 56 * 1024 * 1024
LANE = 128
HALO_ROWS = 8
BF16_SUBLANES = 16
MOBA_KV_GROUP = 2
MOBA_HEADS_PER_STEP = 2
LOG2E = math.log2(math.e)
VT_ROWS = MOBA_HEAD_DIM + BF16_SUBLANES


def _cparams(*sem):
    return pltpu.CompilerParams(dimension_semantics=sem,
                                vmem_limit_bytes=VMEM_LIMIT_BYTES)


def _rmsnorm_rows(x, w):
    ms = jnp.mean(x * x, axis=-1, keepdims=True)
    return x * lax.rsqrt(ms + EPS) * w


def _dot(a, b):
    return jnp.dot(a, b, preferred_element_type=F32)


def _dot_nt(a, b):
    return lax.dot_general(a, b, (((1,), (1,)), ((), ())), preferred_element_type=F32)


def _dot_tn(a, b):
    return lax.dot_general(a, b, (((0,), (0,)), ((), ())), preferred_element_type=F32)


def _gla_in_kernel(x_ref, nw_ref, w_ref, w1_ref, w2_ref, b_ref, proj_ref, la_ref, xn_scr):
    @pl.when(pl.program_id(1) == 0)
    def _():
        xn = _rmsnorm_rows(x_ref[...], nw_ref[...]).astype(BF16)
        xn_scr[...] = xn
        r = _dot(xn, w1_ref[...])
        gk = _dot(r.astype(BF16), w2_ref[...]) + b_ref[...]
        log_sig = jnp.minimum(gk, 0.0) - jnp.log1p(jnp.exp(-jnp.abs(gk)))
        la_ref[...] = log_sig * (1.0 / GLA_GATE_NORMALIZER)

    proj_ref[...] = _dot(xn_scr[...], w_ref[...])


def _gla_in(x, norm_w, w_in, w1p, w2p, gk_b, *, tm=1024, tn=512):
    S, D = x.shape
    N = w_in.shape[1]
    KD = w2p.shape[1]
    return pl.pallas_call(
        _gla_in_kernel,
        grid=(S // tm, N // tn),
        in_specs=[
            pl.BlockSpec((tm, D), lambda i, j: (i, 0)),
            pl.BlockSpec((1, D), lambda i, j: (0, 0)),
            pl.BlockSpec((D, tn), lambda i, j: (0, j)),
            pl.BlockSpec((D, LANE), lambda i, j: (0, 0)),
            pl.BlockSpec((LANE, KD), lambda i, j: (0, 0)),
            pl.BlockSpec((1, KD), lambda i, j: (0, 0)),
        ],
        out_specs=[
            pl.BlockSpec((tm, tn), lambda i, j: (i, j)),
            pl.BlockSpec((tm, KD), lambda i, j: (i, 0)),
        ],
        out_shape=[jax.ShapeDtypeStruct((S, N), F32),
                   jax.ShapeDtypeStruct((S, KD), F32)],
        scratch_shapes=[pltpu.VMEM((tm, D), BF16)],
        compiler_params=_cparams("arbitrary", "arbitrary"),
        name="gla_in",
    )(x, norm_w, w_in, w1p, w2p, gk_b)


def _gla_core_kernel(q_ref, k_ref, v_ref, g_ref, la_ref, onw_ref, o_ref, st_ref, *, n_chunks, dk):
    C = GLA_CHUNK

    @pl.when(pl.program_id(1) == 0)
    def _():
        st_ref[...] = jnp.zeros_like(st_ref)

    row = lax.broadcasted_iota(jnp.int32, (C, C), 0)
    col = lax.broadcasted_iota(jnp.int32, (C, C), 1)
    causal = row >= col
    tril = causal.astype(BF16)
    q_scale = dk ** -0.5

    def chunk(c, carry):
        r0 = pl.multiple_of(c * C, C)
        rows = pl.ds(r0, C)
        la = la_ref[rows, :]
        la_hi = la.astype(BF16)
        la_lo = (la - la_hi.astype(F32)).astype(BF16)
        b = _dot(tril, la_hi) + _dot(tril, la_lo)
        b_last = b[C - 1:C, :]
        q = q_ref[rows, :]
        k = k_ref[rows, :]
        v = v_ref[rows, :].astype(BF16)
        q_dec = ((q * q_scale) * jnp.exp(b)).astype(BF16)
        k_inv = (k * jnp.exp(-b)).astype(BF16)
        k_end = (k * jnp.exp(b_last - b)).astype(BF16)
        chunk_decay = jnp.exp(b_last)
        attn = jnp.where(causal, _dot_nt(q_dec, k_inv), 0.0)
        st = st_ref[...]
        o = _dot(attn.astype(BF16), v) + _dot_nt(q_dec, st.astype(BF16))
        st_ref[...] = st * chunk_decay + _dot_tn(v, k_end)
        y = _rmsnorm_rows(o, onw_ref[...])
        g = g_ref[rows, :]
        y = y * (g * (1.0 / (1.0 + jnp.exp(-g))))
        o_ref[rows, :] = y.astype(BF16)
        return carry

    lax.fori_loop(0, n_chunks, chunk, 0)


def _gla_core(proj, log_a, o_norm_w, *, tile=512):
    S = proj.shape[0]
    H = GLA_HEADS
    KD = log_a.shape[1]
    dk = KD // H
    VD = (proj.shape[1] - 2 * KD) // 2
    dv = VD // H
    kq, kv_ = KD // dk, VD // dv
    return pl.pallas_call(
        functools.partial(_gla_core_kernel, n_chunks=tile // GLA_CHUNK, dk=dk),
        grid=(H, S // tile),
        in_specs=[
            pl.BlockSpec((tile, dk), lambda h, t: (t, h)),
            pl.BlockSpec((tile, dk), lambda h, t: (t, kq + h)),
            pl.BlockSpec((tile, dv), lambda h, t: (t, (2 * KD) // dv + h)),
            pl.BlockSpec((tile, dv), lambda h, t: (t, (2 * KD + VD) // dv + h)),
            pl.BlockSpec((tile, dk), lambda h, t: (t, h)),
            pl.BlockSpec((1, dv), lambda h, t: (0, 0)),
        ],
        out_specs=pl.BlockSpec((tile, dv), lambda h, t: (t, h)),
        out_shape=jax.ShapeDtypeStruct((S, VD), BF16),
        scratch_shapes=[pltpu.VMEM((dv, dk), F32)],
        compiler_params=_cparams("arbitrary", "arbitrary"),
        name="gla_core",
    )(proj, proj, proj, proj, log_a, o_norm_w)


def _mm_res_kernel(a_ref, w_ref, r_ref, o_ref):
    o_ref[...] = r_ref[...] + _dot(a_ref[...], w_ref[...])


def _mm_res(a, w, res, *, tm=1024, tn=1024):
    S, K = a.shape
    N = w.shape[1]
    return pl.pallas_call(
        _mm_res_kernel,
        grid=(S // tm, N // tn),
        in_specs=[
            pl.BlockSpec((tm, K), lambda i, j: (i, 0)),
            pl.BlockSpec((K, tn), lambda i, j: (0, j)),
            pl.BlockSpec((tm, tn), lambda i, j: (i, j)),
        ],
        out_specs=pl.BlockSpec((tm, tn), lambda i, j: (i, j)),
        out_shape=jax.ShapeDtypeStruct((S, N), F32),
        compiler_params=_cparams("arbitrary", "arbitrary"),
        name="mm_res",
    )(a, w, res)


def _ffn_kernel(x_ref, nw_ref, wa_ref, wu_ref, cwa_ref, cwu_ref, cba_ref, cbu_ref, wd_ref,
                o_ref, xn_scr, halo_a, halo_u):
    i = pl.program_id(0)
    j = pl.program_id(1)

    @pl.when(j == 0)
    def _():
        x = x_ref[...]
        xn_scr[...] = _rmsnorm_rows(x, nw_ref[...]).astype(BF16)
        o_ref[...] = x

    @pl.when(i == 0)
    def _():
        halo_a[j] = jnp.zeros(halo_a.shape[1:], F32)
        halo_u[j] = jnp.zeros(halo_u.shape[1:], F32)

    xn = xn_scr[...]
    tm = xn.shape[0]
    tf = wa_ref.shape[1]
    row = lax.broadcasted_iota(jnp.int32, (tm, tf), 0)
    is_row0 = row == 0
    is_row1 = row == 1

    def conv(h, halo_ref, cw_ref, cb_ref):
        halo = halo_ref[j]
        prev1 = halo[HALO_ROWS - 1:HALO_ROWS, :]
        prev2 = halo[HALO_ROWS - 2:HALO_ROWS - 1, :]
        h1 = jnp.where(is_row0, prev1, pltpu.roll(h, 1, 0))
        h2 = jnp.where(is_row0, prev2, jnp.where(is_row1, prev1, pltpu.roll(h, 2, 0)))
        halo_ref[j] = h[tm - HALO_ROWS:tm, :]
        return cb_ref[...] + cw_ref[0:1, :] * h2 + cw_ref[1:2, :] * h1 + cw_ref[2:3, :] * h

    ca = conv(_dot(xn, wa_ref[...]), halo_a, cwa_ref, cba_ref)
    cu = conv(_dot(xn, wu_ref[...]), halo_u, cwu_ref, cbu_ref)
    act = (ca * (1.0 / (1.0 + jnp.exp(-ca)))) * cu
    o_ref[...] += _dot(act.astype(BF16), wd_ref[...])


def _conv_ffn(x, norm_w, w_up, conv_w, conv_b, w_down, *, tm=512, tf=512):
    S, D = x.shape
    Fh = w_down.shape[0]
    nF = Fh // tf
    return pl.pallas_call(
        _ffn_kernel,
        grid=(S // tm, nF),
        in_specs=[
            pl.BlockSpec((tm, D), lambda i, j: (i, 0)),
            pl.BlockSpec((1, D), lambda i, j: (0, 0)),
            pl.BlockSpec((D, tf), lambda i, j: (0, j)),
            pl.BlockSpec((D, tf), lambda i, j: (0, nF + j)),
            pl.BlockSpec((CONV_WIDTH, tf), lambda i, j: (0, j)),
            pl.BlockSpec((CONV_WIDTH, tf), lambda i, j: (0, nF + j)),
            pl.BlockSpec((1, tf), lambda i, j: (0, j)),
            pl.BlockSpec((1, tf), lambda i, j: (0, nF + j)),
            pl.BlockSpec((tf, D), lambda i, j: (j, 0)),
        ],
        out_specs=pl.BlockSpec((tm, D), lambda i, j: (i, 0)),
        out_shape=jax.ShapeDtypeStruct((S, D), F32),
        scratch_shapes=[pltpu.VMEM((tm, D), BF16),
                        pltpu.VMEM((nF, HALO_ROWS, tf), F32),
                        pltpu.VMEM((nF, HALO_ROWS, tf), F32)],
        compiler_params=_cparams("arbitrary", "arbitrary"),
        name="conv_ffn",
    )(x, norm_w, w_up, w_up, conv_w, conv_w, conv_b, conv_b, w_down)


def _proj_headnorm_kernel(x_ref, nw_ref, w_ref, hw_ref, o_ref, xn_scr, *, scale):
    @pl.when(pl.program_id(1) == 0)
    def _():
        xn_scr[...] = _rmsnorm_rows(x_ref[...], nw_ref[...]).astype(BF16)

    y = _dot(xn_scr[...], w_ref[...])
    Dh = hw_ref.shape[1]
    for hh in range(y.shape[1] // Dh):
        cols = slice(hh * Dh, (hh + 1) * Dh)
        yn = _rmsnorm_rows(y[:, cols], hw_ref[...])
        if scale is not None:
            yn = yn * scale
        o_ref[:, cols] = yn.astype(BF16)


def _proj_headnorm(x, norm_w, w, head_w, n_out, *, scale=None, tm=1024, tn=512):
    S, D = x.shape
    Dh = head_w.shape[1]
    return pl.pallas_call(
        functools.partial(_proj_headnorm_kernel, scale=scale),
        grid=(S // tm, n_out // tn),
        in_specs=[
            pl.BlockSpec((tm, D), lambda i, j: (i, 0)),
            pl.BlockSpec((1, D), lambda i, j: (0, 0)),
            pl.BlockSpec((D, tn), lambda i, j: (0, j)),
            pl.BlockSpec((1, Dh), lambda i, j: (0, 0)),
        ],
        out_specs=pl.BlockSpec((tm, tn), lambda i, j: (i, j)),
        out_shape=jax.ShapeDtypeStruct((S, n_out), BF16),
        scratch_shapes=[pltpu.VMEM((tm, D), BF16)],
        compiler_params=_cparams("arbitrary", "arbitrary"),
        name="proj_headnorm",
    )(x, norm_w, w, head_w)


def _vproj_kernel(x_ref, nw_ref, w_ref, vt_ref, xn_scr):
    @pl.when(pl.program_id(1) == 0)
    def _():
        xn_scr[...] = _rmsnorm_rows(x_ref[...], nw_ref[...]).astype(BF16)

    y = _dot(xn_scr[...], w_ref[...])
    n_h, n_b, rows, BLK = vt_ref.shape
    Dh = MOBA_HEAD_DIM
    ones = jnp.ones((rows - Dh, BLK), BF16)
    for hh in range(n_h):
        for bb in range(n_b):
            blk = y[bb * BLK:(bb + 1) * BLK, hh * Dh:(hh + 1) * Dh]
            vt_ref[hh, bb, 0:Dh, :] = blk.T.astype(BF16)
            vt_ref[hh, bb, Dh:rows, :] = ones


def _vproj(x, norm_w, w_kv, col0, *, tm=1024, tn=512):
    S, D = x.shape
    H, Dh, BLK = MOBA_HEADS, MOBA_HEAD_DIM, MOBA_BLOCK
    off = col0 // tn
    return pl.pallas_call(
        _vproj_kernel,
        grid=(S // tm, (H * Dh) // tn),
        in_specs=[
            pl.BlockSpec((tm, D), lambda i, j: (i, 0)),
            pl.BlockSpec((1, D), lambda i, j: (0, 0)),
            pl.BlockSpec((D, tn), lambda i, j: (0, off + j)),
        ],
        out_specs=pl.BlockSpec((tn // Dh, tm // BLK, VT_ROWS, BLK), lambda i, j: (j, i, 0, 0)),
        out_shape=jax.ShapeDtypeStruct((H, S // BLK, VT_ROWS, BLK), BF16),
        scratch_shapes=[pltpu.VMEM((tm, D), BF16)],
        compiler_params=_cparams("arbitrary", "arbitrary"),
        name="v_proj",
    )(x, norm_w, w_kv)


def _kmean_kernel(k_ref, o_ref):
    rows, cols = k_ref.shape
    k = k_ref[...].astype(F32).reshape(rows // MOBA_BLOCK, MOBA_BLOCK, cols)
    o_ref[...] = jnp.sum(k, axis=1) * (1.0 / MOBA_BLOCK)


def _kmean(k, *, tr=2048, tc=512):
    S, N = k.shape
    return pl.pallas_call(
        _kmean_kernel,
        grid=(S // tr, N // tc),
        in_specs=[pl.BlockSpec((tr, tc), lambda i, j: (i, j))],
        out_specs=pl.BlockSpec((tr // MOBA_BLOCK, tc), lambda i, j: (i, j)),
        out_shape=jax.ShapeDtypeStruct((S // MOBA_BLOCK, N), F32),
        compiler_params=_cparams("arbitrary", "arbitrary"),
        name="kmean",
    )(k)


def _gate_kernel(q_ref, km_ref, mask_ref):
    nb = km_ref.shape[0]
    tq = q_ref.shape[0]
    g = _dot_nt(km_ref[...].astype(BF16), q_ref[...])
    blk = lax.broadcasted_iota(jnp.int32, (nb, tq), 0)
    pos = pl.program_id(1) * tq + lax.broadcasted_iota(jnp.int32, (nb, tq), 1)
    own = lax.shift_right_logical(pos, int(math.log2(MOBA_BLOCK)))
    neg = -jnp.inf
    v = jnp.where(blk < own, g, neg)
    sel = jnp.zeros((nb, tq), jnp.bool_)
    for r in range(min(MOBA_TOPK, nb)):
        m = jnp.max(v, axis=0, keepdims=True)
        first = jnp.min(jnp.where(v == m, blk, nb), axis=0, keepdims=True)
        onehot = blk == first
        sel = sel | (onehot & (own > r))
        v = jnp.where(onehot, neg, v)
    mask_ref[0] = jnp.where(sel, 0.0, neg)


def _gate_mask(q, kmean, *, tq=2048):
    S = q.shape[0]
    H, Dh = MOBA_HEADS, MOBA_HEAD_DIM
    nb = kmean.shape[0]
    return pl.pallas_call(
        _gate_kernel,
        grid=(H, S // tq),
        in_specs=[
            pl.BlockSpec((tq, Dh), lambda h, t: (t, h)),
            pl.BlockSpec((nb, Dh), lambda h, t: (0, h)),
        ],
        out_specs=pl.BlockSpec((1, nb, tq), lambda h, t: (h, 0, t)),
        out_shape=jax.ShapeDtypeStruct((H, nb, S), F32),
        compiler_params=_cparams("arbitrary", "arbitrary"),
        name="gate_topk",
    )(q, kmean)


def _rel_bias_tile(tbl_ref, h, dist):
    max_exact = REL_BUCKETS // 2
    n = jnp.maximum(dist, 0)
    large = max_exact + (jnp.log(jnp.maximum(n, 1).astype(F32) / max_exact)
                         / math.log(REL_MAX_DISTANCE / max_exact)
                         * (REL_BUCKETS - max_exact)).astype(jnp.int32)
    large = jnp.minimum(large, REL_BUCKETS - 1)
    bucket = jnp.where(n < max_exact, n, large)
    out = jnp.zeros(dist.shape, F32)
    for b in range(REL_BUCKETS):
        out = jnp.where(bucket == b, tbl_ref[h, b], out)
    return out


def _moba_kernel(tbl_ref, q_ref, k_ref, vt_ref, mask_ref, o_ref,
                 bown_ref, bprev_ref, sa_ref, sb_ref):
    BLK, G, Dh = MOBA_BLOCK, MOBA_KV_GROUP, MOBA_HEAD_DIM
    NH, nb = vt_ref.shape[0], vt_ref.shape[1]
    STEP = 2 * G
    head0 = pl.program_id(0) * NH
    i = pl.program_id(1)
    neg = -jnp.inf

    @pl.when(i == 0)
    def _():
        key = lax.broadcasted_iota(jnp.int32, (BLK, BLK), 0)
        qry = lax.broadcasted_iota(jnp.int32, (BLK, BLK), 1)
        dist = qry - key
        for hh in range(NH):
            own = _rel_bias_tile(tbl_ref, head0 + hh, dist) * LOG2E
            bown_ref[hh] = jnp.where(dist >= 0, own, neg)
            bprev_ref[hh] = _rel_bias_tile(tbl_ref, head0 + hh, dist + BLK) * LOG2E

    q = [q_ref[:, hh * Dh:(hh + 1) * Dh] for hh in range(NH)]

    def k_rows(hh, j, n):
        return k_ref[pl.ds(pl.multiple_of(j * BLK, BLK), n * BLK), hh * Dh:(hh + 1) * Dh]

    def colmax(s):
        return jnp.max(s, axis=0, keepdims=True)

    def put_scores(s_ref, j0):
        for hh in range(NH):
            s_ref[hh] = _dot_nt(k_rows(hh, j0, G), q[hh])

    put_scores(sa_ref, 0)

    jp = jnp.maximum(i - 1, 0)
    carry = []
    for hh in range(NH):
        s_own = _dot_nt(k_rows(hh, i, 1), q[hh]) + bown_ref[hh]
        s_prev = _dot_nt(k_rows(hh, jp, 1), q[hh]) + bprev_ref[hh]
        sel_prev = mask_ref[hh, pl.ds(jp, 1), :]
        m = jnp.maximum(colmax(s_own), colmax(s_prev) + sel_prev)
        p_own = jnp.exp2(s_own - m).astype(BF16)
        p_prev = jnp.exp2(s_prev - (m - sel_prev)).astype(BF16)
        carry += [m, _dot(vt_ref[hh, i], p_own) + _dot(vt_ref[hh, jp], p_prev)]

    far_bias = [tbl_ref[head0 + hh, REL_BUCKETS - 1] * LOG2E for hh in range(NH)]

    def update(s_ref, j0, carry):
        out = []
        for hh in range(NH):
            m, acc = carry[2 * hh], carry[2 * hh + 1]
            sel = []
            m_new = m
            for g in range(G):
                row = (mask_ref[hh, pl.ds(j0 + g, 1), :]
                       + jnp.where(j0 + g < i - 1, far_bias[hh], neg))
                sel.append(row)
                m_new = jnp.maximum(m_new, colmax(s_ref[hh, g * BLK:(g + 1) * BLK, :]) + row)
            acc = acc * jnp.exp2(m - m_new)
            for g in range(G):
                p = jnp.exp2(s_ref[hh, g * BLK:(g + 1) * BLK, :] - (m_new - sel[g])).astype(BF16)
                acc = acc + _dot(vt_ref[hh, j0 + g], p)
            out += [m_new, acc]
        return out

    def trip(t, carry):
        j0 = t * STEP
        put_scores(sb_ref, j0 + G)
        carry = update(sa_ref, j0, list(carry))
        put_scores(sa_ref, jnp.minimum(j0 + STEP, nb - G))
        carry = update(sb_ref, j0 + G, carry)
        return tuple(carry)

    n_trips = (i + (STEP - 2)) // STEP
    carry = lax.fori_loop(0, n_trips, trip, tuple(carry))
    for hh in range(NH):
        acc = carry[2 * hh + 1]
        o_ref[:, hh * Dh:(hh + 1) * Dh] = (acc[:Dh] / acc[Dh:Dh + 1]).T.astype(BF16)


def _moba_attn(tbl, q, k, vt, mask):
    S = q.shape[0]
    H, Dh, BLK = MOBA_HEADS, MOBA_HEAD_DIM, MOBA_BLOCK
    nb = S // BLK
    NH, G = MOBA_HEADS_PER_STEP, MOBA_KV_GROUP
    return pl.pallas_call(
        _moba_kernel,
        grid=(H // NH, nb),
        in_specs=[
            pl.BlockSpec(memory_space=pltpu.SMEM),
            pl.BlockSpec((BLK, NH * Dh), lambda h, i: (i, h)),
            pl.BlockSpec((S, NH * Dh), lambda h, i: (0, h)),
            pl.BlockSpec((NH, nb, VT_ROWS, BLK), lambda h, i: (h, 0, 0, 0)),
            pl.BlockSpec((NH, nb, BLK), lambda h, i: (h, 0, i)),
        ],
        out_specs=pl.BlockSpec((BLK, NH * Dh), lambda h, i: (i, h)),
        out_shape=jax.ShapeDtypeStruct((S, H * Dh), BF16),
        scratch_shapes=[pltpu.VMEM((NH, BLK, BLK), F32), pltpu.VMEM((NH, BLK, BLK), F32),
                        pltpu.VMEM((NH, G * BLK, BLK), F32), pltpu.VMEM((NH, G * BLK, BLK), F32)],
        compiler_params=_cparams("arbitrary", "arbitrary"),
        name="moba_attn",
    )(tbl, q, k, vt, mask)


def _row(v):
    return v.reshape(1, -1)


def kernel(x, gla_norm, gla_w_in, gla_gk_w1, gla_gk_w2, gla_gk_b, gla_o_norm, gla_w_out,
           kv_norm, kv_w, k_norm_w, moba_norm, moba_w_q, moba_q_norm, moba_w_out, rel_bias,
           ffn_norm, ffn_w_up, ffn_conv_w, ffn_conv_b, ffn_w_down):
    B, S, D = x.shape
    assert B == 1
    depth = ffn_norm.shape[0]
    n_a = gla_norm.shape[0]
    h = x[0]
    HD = MOBA_HEADS * MOBA_HEAD_DIM
    k_bf = vt = kmean = None
    tbl = rel_bias.T

    for layer in range(depth):
        if layer < n_a:
            a = layer
            R = gla_gk_w1.shape[2]
            w1p = jnp.pad(gla_gk_w1[a].astype(BF16), ((0, 0), (0, LANE - R)))
            w2p = jnp.pad(gla_gk_w2[a].astype(BF16), ((0, LANE - R), (0, 0)))
            proj, log_a = _gla_in(h, _row(gla_norm[a]), gla_w_in[a].astype(BF16), w1p, w2p,
                                  _row(gla_gk_b[a]))
            o = _gla_core(proj, log_a, _row(gla_o_norm[a]))
            h = _mm_res(o, gla_w_out[a].astype(BF16), h)
        else:
            b = layer - n_a
            if k_bf is None:
                kv_bf = kv_w.astype(BF16)
                k_bf = _proj_headnorm(h, _row(kv_norm), kv_bf, _row(k_norm_w), HD)
                vt = _vproj(h, _row(kv_norm), kv_bf, HD)
                kmean = _kmean(k_bf)
            q = _proj_headnorm(h, _row(moba_norm[b]), moba_w_q[b].astype(BF16),
                               _row(moba_q_norm[b]), HD,
                               scale=MOBA_HEAD_DIM ** -0.5 * LOG2E)
            mask = _gate_mask(q, kmean)
            o = _moba_attn(tbl, q, k_bf, vt, mask)
            h = _mm_res(o, moba_w_out[b].astype(BF16), h)
        h = _conv_ffn(h, _row(ffn_norm[layer]), ffn_w_up[layer].astype(BF16),
                      ffn_conv_w[layer], _row(ffn_conv_b[layer]), ffn_w_down[layer].astype(BF16))
    return h[None]
```

```python
import functools
import math

import jax
import jax.numpy as jnp
from jax import lax
from jax.experimental import pallas as pl
from jax.experimental.pallas import tpu as pltpu

F32 = jnp.float32
BF16 = jnp.bfloat16

EPS = 1e-6
GLA_HEADS = 4
GLA_CHUNK = 64
GLA_GATE_NORMALIZER = 16.0
MOBA_HEADS = 16
MOBA_HEAD_DIM = 128
MOBA_BLOCK = 256
MOBA_TOPK = 3
REL_BUCKETS = 32
REL_MAX_DISTANCE = 128
CONV_WIDTH = 3

VMEM_LIMIT_BYTES = 56 * 1024 * 1024
LANE = 128
HALO_ROWS = 8
BF16_SUBLANES = 16
MOBA_KV_GROUP = 2
MOBA_HEADS_PER_STEP = 2
LOG2E = math.log2(math.e)
FFN_ROW_STRIP = 64
VT_ROWS = MOBA_HEAD_DIM + BF16_SUBLANES


def _cparams(*sem):
    return pltpu.CompilerParams(dimension_semantics=sem,
                                vmem_limit_bytes=VMEM_LIMIT_BYTES)


def _rmsnorm_rows(x, w):
    ms = jnp.mean(x * x, axis=-1, keepdims=True)
    return x * lax.rsqrt(ms + EPS) * w


def _dot(a, b):
    return jnp.dot(a, b, preferred_element_type=F32)


def _dot_nt(a, b):
    return lax.dot_general(a, b, (((1,), (1,)), ((), ())), preferred_element_type=F32)


def _dot_tn(a, b):
    return lax.dot_general(a, b, (((0,), (0,)), ((), ())), preferred_element_type=F32)


def _gla_in_kernel(x_ref, nw_ref, w_ref, w1_ref, w2_ref, b_ref, proj_ref, la_ref, xn_scr):
    @pl.when(pl.program_id(1) == 0)
    def _():
        xn = _rmsnorm_rows(x_ref[...], nw_ref[...]).astype(BF16)
        xn_scr[...] = xn
        r = _dot(xn, w1_ref[...])
        gk = _dot(r.astype(BF16), w2_ref[...]) + b_ref[...]
        log_sig = jnp.minimum(gk, 0.0) - jnp.log1p(jnp.exp(-jnp.abs(gk)))
        la_ref[...] = log_sig * (1.0 / GLA_GATE_NORMALIZER)

    proj_ref[...] = _dot(xn_scr[...], w_ref[...])


def _gla_in(x, norm_w, w_in, w1p, w2p, gk_b, *, tm=1024, tn=512):
    S, D = x.shape
    N = w_in.shape[1]
    KD = w2p.shape[1]
    return pl.pallas_call(
        _gla_in_kernel,
        grid=(S // tm, N // tn),
        in_specs=[
            pl.BlockSpec((tm, D), lambda i, j: (i, 0)),
            pl.BlockSpec((1, D), lambda i, j: (0, 0)),
            pl.BlockSpec((D, tn), lambda i, j: (0, j)),
            pl.BlockSpec((D, LANE), lambda i, j: (0, 0)),
            pl.BlockSpec((LANE, KD), lambda i, j: (0, 0)),
            pl.BlockSpec((1, KD), lambda i, j: (0, 0)),
        ],
        out_specs=[
            pl.BlockSpec((tm, tn), lambda i, j: (i, j)),
            pl.BlockSpec((tm, KD), lambda i, j: (i, 0)),
        ],
        out_shape=[jax.ShapeDtypeStruct((S, N), F32),
                   jax.ShapeDtypeStruct((S, KD), F32)],
        scratch_shapes=[pltpu.VMEM((tm, D), BF16)],
        compiler_params=_cparams("arbitrary", "arbitrary"),
        name="gla_in",
    )(x, norm_w, w_in, w1p, w2p, gk_b)


def _gla_core_kernel(q_ref, k_ref, v_ref, g_ref, la_ref, onw_ref, o_ref, st_ref, *, n_chunks):
    C = GLA_CHUNK
    H, dv, dk = st_ref.shape

    @pl.when(pl.program_id(0) == 0)
    def _():
        st_ref[...] = jnp.zeros_like(st_ref)

    row = lax.broadcasted_iota(jnp.int32, (C, C), 0)
    col = lax.broadcasted_iota(jnp.int32, (C, C), 1)
    causal = row >= col
    tril = causal.astype(BF16)
    q_scale = dk ** -0.5

    def chunk(c, carry):
        rows = pl.ds(pl.multiple_of(c * C, C), C)
        for h in range(H):
            kc = slice(h * dk, (h + 1) * dk)
            vc = slice(h * dv, (h + 1) * dv)
            la = la_ref[rows, kc]
            la_hi = la.astype(BF16)
            la_lo = (la - la_hi.astype(F32)).astype(BF16)
            b = _dot(tril, la_hi) + _dot(tril, la_lo)
            b_last = b[C - 1:C, :]
            q = q_ref[rows, kc]
            k = k_ref[rows, kc]
            v = v_ref[rows, vc].astype(BF16)
            q_dec = ((q * q_scale) * jnp.exp(b)).astype(BF16)
            k_inv = (k * jnp.exp(-b)).astype(BF16)
            k_end = (k * jnp.exp(b_last - b)).astype(BF16)
            chunk_decay = jnp.exp(b_last)
            attn = jnp.where(causal, _dot_nt(q_dec, k_inv), 0.0)
            st = st_ref[h]
            o = _dot(attn.astype(BF16), v) + _dot_nt(q_dec, st.astype(BF16))
            st_ref[h] = st * chunk_decay + _dot_tn(v, k_end)
            y = _rmsnorm_rows(o, onw_ref[...])
            g = g_ref[rows, vc]
            y = y * (g * (1.0 / (1.0 + jnp.exp(-g))))
            o_ref[rows, vc] = y.astype(BF16)
        return carry

    lax.fori_loop(0, n_chunks, chunk, 0)


def _gla_core(proj, log_a, o_norm_w, *, tile=256):
    S = proj.shape[0]
    H = GLA_HEADS
    KD = log_a.shape[1]
    VD = (proj.shape[1] - 2 * KD) // 2
    assert VD == 2 * KD
    return pl.pallas_call(
        functools.partial(_gla_core_kernel, n_chunks=tile // GLA_CHUNK),
        grid=(S // tile,),
        in_specs=[
            pl.BlockSpec((tile, KD), lambda t: (t, 0)),
            pl.BlockSpec((tile, KD), lambda t: (t, 1)),
            pl.BlockSpec((tile, VD), lambda t: (t, 1)),
            pl.BlockSpec((tile, VD), lambda t: (t, 2)),
            pl.BlockSpec((tile, KD), lambda t: (t, 0)),
            pl.BlockSpec((1, VD // H), lambda t: (0, 0)),
        ],
        out_specs=pl.BlockSpec((tile, VD), lambda t: (t, 0)),
        out_shape=jax.ShapeDtypeStruct((S, VD), BF16),
        scratch_shapes=[pltpu.VMEM((H, VD // H, KD // H), F32)],
        compiler_params=_cparams("arbitrary"),
        name="gla_core",
    )(proj, proj, proj, proj, log_a, o_norm_w)


def _mm_res_kernel(a_ref, w_ref, r_ref, o_ref):
    o_ref[...] = r_ref[...] + _dot(a_ref[...], w_ref[...])


def _mm_res(a, w, res, *, tm=1024, tn=1024):
    S, K = a.shape
    N = w.shape[1]
    return pl.pallas_call(
        _mm_res_kernel,
        grid=(S // tm, N // tn),
        in_specs=[
            pl.BlockSpec((tm, K), lambda i, j: (i, 0)),
            pl.BlockSpec((K, tn), lambda i, j: (0, j)),
            pl.BlockSpec((tm, tn), lambda i, j: (i, j)),
        ],
        out_specs=pl.BlockSpec((tm, tn), lambda i, j: (i, j)),
        out_shape=jax.ShapeDtypeStruct((S, N), F32),
        compiler_params=_cparams("arbitrary", "arbitrary"),
        name="mm_res",
    )(a, w, res)


def _ffn_kernel(xa_ref, xc_ref, nw_ref, wa_ref, wu_ref, cwa_ref, cwu_ref, cba_ref, cbu_ref, wd_ref,
                o_ref, xn_scr, halo_a, halo_u, ha0, hu0, ha1, hu1, act0, act1, *, n_chunks, n_steps):
    s = pl.program_id(0)
    last = n_steps - 1
    ja = jnp.minimum(s, last) % n_chunks
    jb = jnp.clip(s - 1, 0, last) % n_chunks
    jc = jnp.clip(s - 2, 0, last) % n_chunks

    @pl.when(s == 0)
    def _():
        for ref in (halo_a, halo_u, ha0, hu0, ha1, hu1, act0, act1):
            ref[...] = jnp.zeros(ref.shape, ref.dtype)

    @pl.when((ja == 0) & (s <= last))
    def _():
        xn_scr[...] = _rmsnorm_rows(xa_ref[...], nw_ref[...]).astype(BF16)

    @pl.when(jc == 0)
    def _():
        o_ref[...] = xc_ref[...]

    tm, tf = act0.shape
    top = HALO_ROWS

    def conv_rows(h_ref, cw_ref, cb_ref, r0, n):
        h2 = h_ref[top - 2 + r0:top - 2 + r0 + n, :]
        h1 = h_ref[top - 1 + r0:top - 1 + r0 + n, :]
        h0 = h_ref[top + r0:top + r0 + n, :]
        return cb_ref[...] + cw_ref[0:1, :] * h2 + cw_ref[1:2, :] * h1 + cw_ref[2:3, :] * h0

    def stages(h_out, h_in, act_out, act_in):
        for h_ref, halo_ref in ((h_in[0], halo_a), (h_in[1], halo_u)):
            h_ref[0:top, :] = halo_ref[jb]
            halo_ref[jb] = h_ref[tm:tm + top, :]
        for r0 in range(0, tm, FFN_ROW_STRIP):
            ca = conv_rows(h_in[0], cwa_ref, cba_ref, r0, FFN_ROW_STRIP)
            cu = conv_rows(h_in[1], cwu_ref, cbu_ref, r0, FFN_ROW_STRIP)
            act = (ca * (1.0 / (1.0 + jnp.exp(-ca)))) * cu
            act_out[r0:r0 + FFN_ROW_STRIP, :] = act.astype(BF16)
        xn = xn_scr[...]
        h_out[0][top:top + tm, :] = _dot(xn, wa_ref[...])
        h_out[1][top:top + tm, :] = _dot(xn, wu_ref[...])
        o_ref[...] += _dot(act_in[...], wd_ref[...])

    @pl.when(s % 2 == 0)
    def _():
        stages((ha0, hu0), (ha1, hu1), act0, act1)

    @pl.when(s % 2 == 1)
    def _():
        stages((ha1, hu1), (ha0, hu0), act1, act0)


def _conv_ffn(x, norm_w, w_up, conv_w, conv_b, w_down, *, tm=512, tf=512):
    S, D = x.shape
    Fh = w_down.shape[0]
    nF = Fh // tf
    T = (S // tm) * nF
    last = T - 1

    def a_step(s):
        return jnp.minimum(s, last)

    def b_chunk(s):
        return jnp.clip(s - 1, 0, last) % nF

    def c_step(s):
        return jnp.clip(s - 2, 0, last)

    return pl.pallas_call(
        functools.partial(_ffn_kernel, n_chunks=nF, n_steps=T),
        grid=(T + 2,),
        in_specs=[
            pl.BlockSpec((tm, D), lambda s: (a_step(s) // nF, 0)),
            pl.BlockSpec((tm, D), lambda s: (c_step(s) // nF, 0)),
            pl.BlockSpec((1, D), lambda s: (0, 0)),
            pl.BlockSpec((D, tf), lambda s: (0, a_step(s) % nF)),
            pl.BlockSpec((D, tf), lambda s: (0, nF + a_step(s) % nF)),
            pl.BlockSpec((CONV_WIDTH, tf), lambda s: (0, b_chunk(s))),
            pl.BlockSpec((CONV_WIDTH, tf), lambda s: (0, nF + b_chunk(s))),
            pl.BlockSpec((1, tf), lambda s: (0, b_chunk(s))),
            pl.BlockSpec((1, tf), lambda s: (0, nF + b_chunk(s))),
            pl.BlockSpec((tf, D), lambda s: (c_step(s) % nF, 0)),
        ],
        out_specs=pl.BlockSpec((tm, D), lambda s: (c_step(s) // nF, 0)),
        out_shape=jax.ShapeDtypeStruct((S, D), F32),
        scratch_shapes=[pltpu.VMEM((tm, D), BF16),
                        pltpu.VMEM((nF, HALO_ROWS, tf), F32),
                        pltpu.VMEM((nF, HALO_ROWS, tf), F32),
                        pltpu.VMEM((HALO_ROWS + tm, tf), F32), pltpu.VMEM((HALO_ROWS + tm, tf), F32),
                        pltpu.VMEM((HALO_ROWS + tm, tf), F32), pltpu.VMEM((HALO_ROWS + tm, tf), F32),
                        pltpu.VMEM((tm, tf), BF16), pltpu.VMEM((tm, tf), BF16)],
        compiler_params=_cparams("arbitrary"),
        name="conv_ffn",
    )(x, x, norm_w, w_up, w_up, conv_w, conv_w, conv_b, conv_b, w_down)


def _proj_headnorm_kernel(x_ref, nw_ref, w_ref, hw_ref, o_ref, xn_scr, *, scale):
    @pl.when(pl.program_id(1) == 0)
    def _():
        xn_scr[...] = _rmsnorm_rows(x_ref[...], nw_ref[...]).astype(BF16)

    y = _dot(xn_scr[...], w_ref[...])
    Dh = hw_ref.shape[1]
    for hh in range(y.shape[1] // Dh):
        cols = slice(hh * Dh, (hh + 1) * Dh)
        yn = _rmsnorm_rows(y[:, cols], hw_ref[...])
        if scale is not None:
            yn = yn * scale
        o_ref[:, cols] = yn.astype(BF16)


def _proj_headnorm(x, norm_w, w, head_w, n_out, *, scale=None, tm=1024, tn=512):
    S, D = x.shape
    Dh = head_w.shape[1]
    return pl.pallas_call(
        functools.partial(_proj_headnorm_kernel, scale=scale),
        grid=(S // tm, n_out // tn),
        in_specs=[
            pl.BlockSpec((tm, D), lambda i, j: (i, 0)),
            pl.BlockSpec((1, D), lambda i, j: (0, 0)),
            pl.BlockSpec((D, tn), lambda i, j: (0, j)),
            pl.BlockSpec((1, Dh), lambda i, j: (0, 0)),
        ],
        out_specs=pl.BlockSpec((tm, tn), lambda i, j: (i, j)),
        out_shape=jax.ShapeDtypeStruct((S, n_out), BF16),
        scratch_shapes=[pltpu.VMEM((tm, D), BF16)],
        compiler_params=_cparams("arbitrary", "arbitrary"),
        name="proj_headnorm",
    )(x, norm_w, w, head_w)


def _vproj_kernel(x_ref, nw_ref, w_ref, vt_ref, xn_scr):
    @pl.when(pl.program_id(1) == 0)
    def _():
        xn_scr[...] = _rmsnorm_rows(x_ref[...], nw_ref[...]).astype(BF16)

    y = _dot(xn_scr[...], w_ref[...])
    n_h, n_b, rows, BLK = vt_ref.shape
    Dh = MOBA_HEAD_DIM
    ones = jnp.ones((rows - Dh, BLK), BF16)
    for hh in range(n_h):
        for bb in range(n_b):
            blk = y[bb * BLK:(bb + 1) * BLK, hh * Dh:(hh + 1) * Dh]
            vt_ref[hh, bb, 0:Dh, :] = blk.T.astype(BF16)
            vt_ref[hh, bb, Dh:rows, :] = ones


def _vproj(x, norm_w, w_kv, col0, *, tm=1024, tn=512):
    S, D = x.shape
    H, Dh, BLK = MOBA_HEADS, MOBA_HEAD_DIM, MOBA_BLOCK
    off = col0 // tn
    return pl.pallas_call(
        _vproj_kernel,
        grid=(S // tm, (H * Dh) // tn),
        in_specs=[
            pl.BlockSpec((tm, D), lambda i, j: (i, 0)),
            pl.BlockSpec((1, D), lambda i, j: (0, 0)),
            pl.BlockSpec((D, tn), lambda i, j: (0, off + j)),
        ],
        out_specs=pl.BlockSpec((tn // Dh, tm // BLK, VT_ROWS, BLK), lambda i, j: (j, i, 0, 0)),
        out_shape=jax.ShapeDtypeStruct((H, S // BLK, VT_ROWS, BLK), BF16),
        scratch_shapes=[pltpu.VMEM((tm, D), BF16)],
        compiler_params=_cparams("arbitrary", "arbitrary"),
        name="v_proj",
    )(x, norm_w, w_kv)


def _kmean_kernel(k_ref, o_ref):
    rows, cols = k_ref.shape
    k = k_ref[...].astype(F32).reshape(rows // MOBA_BLOCK, MOBA_BLOCK, cols)
    o_ref[...] = jnp.sum(k, axis=1) * (1.0 / MOBA_BLOCK)


def _kmean(k, *, tr=2048, tc=512):
    S, N = k.shape
    return pl.pallas_call(
        _kmean_kernel,
        grid=(S // tr, N // tc),
        in_specs=[pl.BlockSpec((tr, tc), lambda i, j: (i, j))],
        out_specs=pl.BlockSpec((tr // MOBA_BLOCK, tc), lambda i, j: (i, j)),
        out_shape=jax.ShapeDtypeStruct((S // MOBA_BLOCK, N), F32),
        compiler_params=_cparams("arbitrary", "arbitrary"),
        name="kmean",
    )(k)


def _gate_kernel(q_ref, km_ref, mask_ref):
    nb = km_ref.shape[0]
    tq = q_ref.shape[0]
    g = _dot_nt(km_ref[...].astype(BF16), q_ref[...])
    blk = lax.broadcasted_iota(jnp.int32, (nb, tq), 0)
    pos = pl.program_id(1) * tq + lax.broadcasted_iota(jnp.int32, (nb, tq), 1)
    own = lax.shift_right_logical(pos, int(math.log2(MOBA_BLOCK)))
    neg = -jnp.inf
    v = jnp.where(blk < own, g, neg)
    sel = jnp.zeros((nb, tq), jnp.bool_)
    for r in range(min(MOBA_TOPK, nb)):
        m = jnp.max(v, axis=0, keepdims=True)
        first = jnp.min(jnp.where(v == m, blk, nb), axis=0, keepdims=True)
        onehot = blk == first
        sel = sel | (onehot & (own > r))
        v = jnp.where(onehot, neg, v)
    mask_ref[0] = jnp.where(sel, 0.0, neg)


def _gate_mask(q, kmean, *, tq=2048):
    S = q.shape[0]
    H, Dh = MOBA_HEADS, MOBA_HEAD_DIM
    nb = kmean.shape[0]
    return pl.pallas_call(
        _gate_kernel,
        grid=(H, S // tq),
        in_specs=[
            pl.BlockSpec((tq, Dh), lambda h, t: (t, h)),
            pl.BlockSpec((nb, Dh), lambda h, t: (0, h)),
        ],
        out_specs=pl.BlockSpec((1, nb, tq), lambda h, t: (h, 0, t)),
        out_shape=jax.ShapeDtypeStruct((H, nb, S), F32),
        compiler_params=_cparams("arbitrary", "arbitrary"),
        name="gate_topk",
    )(q, kmean)


def _rel_bias_tile(tbl_ref, h, dist):
    max_exact = REL_BUCKETS // 2
    n = jnp.maximum(dist, 0)
    large = max_exact + (jnp.log(jnp.maximum(n, 1).astype(F32) / max_exact)
                         / math.log(REL_MAX_DISTANCE / max_exact)
                         * (REL_BUCKETS - max_exact)).astype(jnp.int32)
    large = jnp.minimum(large, REL_BUCKETS - 1)
    bucket = jnp.where(n < max_exact, n, large)
    out = jnp.zeros(dist.shape, F32)
    for b in range(REL_BUCKETS):
        out = jnp.where(bucket == b, tbl_ref[h, b], out)
    return out


def _moba_kernel(tbl_ref, q_ref, k_ref, vt_ref, mask_ref, o_ref,
                 bown_ref, bprev_ref, sa_ref, sb_ref):
    BLK, G, Dh = MOBA_BLOCK, MOBA_KV_GROUP, MOBA_HEAD_DIM
    NH, nb = vt_ref.shape[0], vt_ref.shape[1]
    STEP = 2 * G
    head0 = pl.program_id(0) * NH
    i = pl.program_id(1)
    neg = -jnp.inf

    @pl.when(i == 0)
    def _():
        key = lax.broadcasted_iota(jnp.int32, (BLK, BLK), 0)
        qry = lax.broadcasted_iota(jnp.int32, (BLK, BLK), 1)
        dist = qry - key
        for hh in range(NH):
            own = _rel_bias_tile(tbl_ref, head0 + hh, dist) * LOG2E
            bown_ref[hh] = jnp.where(dist >= 0, own, neg)
            bprev_ref[hh] = _rel_bias_tile(tbl_ref, head0 + hh, dist + BLK) * LOG2E

    q = [q_ref[:, hh * Dh:(hh + 1) * Dh] for hh in range(NH)]

    def k_rows(hh, j, n):
        return k_ref[pl.ds(pl.multiple_of(j * BLK, BLK), n * BLK), hh * Dh:(hh + 1) * Dh]

    def colmax(s):
        return jnp.max(s, axis=0, keepdims=True)

    def put_scores(s_ref, j0):
        for hh in range(NH):
            s_ref[hh] = _dot_nt(k_rows(hh, j0, G), q[hh])

    put_scores(sa_ref, 0)

    jp = jnp.maximum(i - 1, 0)
    carry = []
    for hh in range(NH):
        s_own = _dot_nt(k_rows(hh, i, 1), q[hh]) + bown_ref[hh]
        s_prev = _dot_nt(k_rows(hh, jp, 1), q[hh]) + bprev_ref[hh]
        sel_prev = mask_ref[hh, pl.ds(jp, 1), :]
        m = jnp.maximum(colmax(s_own), colmax(s_prev) + sel_prev)
        p_own = jnp.exp2(s_own - m).astype(BF16)
        p_prev = jnp.exp2(s_prev - (m - sel_prev)).astype(BF16)
        carry += [m, _dot(vt_ref[hh, i], p_own) + _dot(vt_ref[hh, jp], p_prev)]

    far_bias = [tbl_ref[head0 + hh, REL_BUCKETS - 1] * LOG2E for hh in range(NH)]

    def update(s_ref, j0, carry):
        out = []
        for hh in range(NH):
            m, acc = carry[2 * hh], carry[2 * hh + 1]
            sel = []
            m_new = m
            for g in range(G):
                row = (mask_ref[hh, pl.ds(j0 + g, 1), :]
                       + jnp.where(j0 + g < i - 1, far_bias[hh], neg))
                sel.append(row)
                m_new = jnp.maximum(m_new, colmax(s_ref[hh, g * BLK:(g + 1) * BLK, :]) + row)
            acc = acc * jnp.exp2(m - m_new)
            for g in range(G):
                p = jnp.exp2(s_ref[hh, g * BLK:(g + 1) * BLK, :] - (m_new - sel[g])).astype(BF16)
                acc = acc + _dot(vt_ref[hh, j0 + g], p)
            out += [m_new, acc]
        return out

    def trip(t, carry):
        j0 = t * STEP
        put_scores(sb_ref, j0 + G)
        carry = update(sa_ref, j0, list(carry))
        put_scores(sa_ref, jnp.minimum(j0 + STEP, nb - G))
        carry = update(sb_ref, j0 + G, carry)
        return tuple(carry)

    n_trips = (i + (STEP - 2)) // STEP
    carry = lax.fori_loop(0, n_trips, trip, tuple(carry))
    for hh in range(NH):
        acc = carry[2 * hh + 1]
        o_ref[:, hh * Dh:(hh + 1) * Dh] = (acc[:Dh] / acc[Dh:Dh + 1]).T.astype(BF16)


def _moba_attn(tbl, q, k, vt, mask):
    S = q.shape[0]
    H, Dh, BLK = MOBA_HEADS, MOBA_HEAD_DIM, MOBA_BLOCK
    nb = S // BLK
    NH, G = MOBA_HEADS_PER_STEP, MOBA_KV_GROUP
    return pl.pallas_call(
        _moba_kernel,
        grid=(H // NH, nb),
        in_specs=[
            pl.BlockSpec(memory_space=pltpu.SMEM),
            pl.BlockSpec((BLK, NH * Dh), lambda h, i: (i, h)),
            pl.BlockSpec((S, NH * Dh), lambda h, i: (0, h)),
            pl.BlockSpec((NH, nb, VT_ROWS, BLK), lambda h, i: (h, 0, 0, 0)),
            pl.BlockSpec((NH, nb, BLK), lambda h, i: (h, 0, i)),
        ],
        out_specs=pl.BlockSpec((BLK, NH * Dh), lambda h, i: (i, h)),
        out_shape=jax.ShapeDtypeStruct((S, H * Dh), BF16),
        scratch_shapes=[pltpu.VMEM((NH, BLK, BLK), F32), pltpu.VMEM((NH, BLK, BLK), F32),
                        pltpu.VMEM((NH, G * BLK, BLK), F32), pltpu.VMEM((NH, G * BLK, BLK), F32)],
        compiler_params=_cparams("arbitrary", "arbitrary"),
        name="moba_attn",
    )(tbl, q, k, vt, mask)


def _row(v):
    return v.reshape(1, -1)


def kernel(x, gla_norm, gla_w_in, gla_gk_w1, gla_gk_w2, gla_gk_b, gla_o_norm, gla_w_out,
           kv_norm, kv_w, k_norm_w, moba_norm, moba_w_q, moba_q_norm, moba_w_out, rel_bias,
           ffn_norm, ffn_w_up, ffn_conv_w, ffn_conv_b, ffn_w_down):
    B, S, D = x.shape
    assert B == 1
    depth = ffn_norm.shape[0]
    n_a = gla_norm.shape[0]
    h = x[0]
    HD = MOBA_HEADS * MOBA_HEAD_DIM
    k_bf = vt = kmean = None
    tbl = rel_bias.T

    for layer in range(depth):
        if layer < n_a:
            a = layer
            R = gla_gk_w1.shape[2]
            w1p = jnp.pad(gla_gk_w1[a].astype(BF16), ((0, 0), (0, LANE - R)))
            w2p = jnp.pad(gla_gk_w2[a].astype(BF16), ((0, LANE - R), (0, 0)))
            proj, log_a = _gla_in(h, _row(gla_norm[a]), gla_w_in[a].astype(BF16), w1p, w2p,
                                  _row(gla_gk_b[a]))
            o = _gla_core(proj, log_a, _row(gla_o_norm[a]))
            h = _mm_res(o, gla_w_out[a].astype(BF16), h)
        else:
            b = layer - n_a
            if k_bf is None:
                kv_bf = kv_w.astype(BF16)
                k_bf = _proj_headnorm(h, _row(kv_norm), kv_bf, _row(k_norm_w), HD)
                vt = _vproj(h, _row(kv_norm), kv_bf, HD)
                kmean = _kmean(k_bf)
            q = _proj_headnorm(h, _row(moba_norm[b]), moba_w_q[b].astype(BF16),
                               _row(moba_q_norm[b]), HD,
                               scale=MOBA_HEAD_DIM ** -0.5 * LOG2E)
            mask = _gate_mask(q, kmean)
            o = _moba_attn(tbl, q, k_bf, vt, mask)
            h = _mm_res(o, moba_w_out[b].astype(BF16), h)
        h = _conv_ffn(h, _row(ffn_norm[layer]), ffn_w_up[layer].astype(BF16),
                      ffn_conv_w[layer], _row(ffn_conv_b[layer]), ffn_w_down[layer].astype(BF16))
    return h[None]
```

```python
import functools
import math

import jax
import jax.numpy as jnp
from jax import lax
from jax.experimental import pallas as pl
from jax.experimental.pallas import tpu as pltpu

F32 = jnp.float32
BF16 = jnp.bfloat16

EPS = 1e-6
GLA_HEADS = 4
GLA_CHUNK = 64
GLA_GATE_NORMALIZER = 16.0
MOBA_HEADS = 16
MOBA_HEAD_DIM = 128
MOBA_BLOCK = 256
MOBA_TOPK = 3
REL_BUCKETS = 32
REL_MAX_DISTANCE = 128
CONV_WIDTH = 3

VMEM_LIMIT_BYTES = 56 * 1024 * 1024
LANE = 128
HALO_ROWS = 8
BF16_SUBLANES = 16
MOBA_KV_GROUP = 2
MOBA_HEADS_PER_STEP = 2
LOG2E = math.log2(math.e)
FFN_ROW_STRIP = 64
VT_ROWS = MOBA_HEAD_DIM + BF16_SUBLANES


def _cparams(*sem):
    return pltpu.CompilerParams(dimension_semantics=sem,
                                vmem_limit_bytes=VMEM_LIMIT_BYTES)


def _rmsnorm_rows(x, w):
    ms = jnp.mean(x * x, axis=-1, keepdims=True)
    return x * lax.rsqrt(ms + EPS) * w


def _dot(a, b):
    return jnp.dot(a, b, preferred_element_type=F32)


def _dot_nt(a, b):
    return lax.dot_general(a, b, (((1,), (1,)), ((), ())), preferred_element_type=F32)


def _dot_tn(a, b):
    return lax.dot_general(a, b, (((0,), (0,)), ((), ())), preferred_element_type=F32)


def _gla_in_kernel(x_ref, nw_ref, w_ref, w1_ref, w2_ref, b_ref, proj_ref, la_ref, xn_scr):
    @pl.when(pl.program_id(1) == 0)
    def _():
        xn = _rmsnorm_rows(x_ref[...], nw_ref[...]).astype(BF16)
        xn_scr[...] = xn
        r = _dot(xn, w1_ref[...])
        gk = _dot(r.astype(BF16), w2_ref[...]) + b_ref[...]
        log_sig = jnp.minimum(gk, 0.0) - jnp.log1p(jnp.exp(-jnp.abs(gk)))
        la_ref[...] = log_sig * (1.0 / GLA_GATE_NORMALIZER)

    proj_ref[...] = _dot(xn_scr[...], w_ref[...])


def _gla_in(x, norm_w, w_in, w1p, w2p, gk_b, *, tm=1024, tn=512):
    S, D = x.shape
    N = w_in.shape[1]
    KD = w2p.shape[1]
    return pl.pallas_call(
        _gla_in_kernel,
        grid=(S // tm, N // tn),
        in_specs=[
            pl.BlockSpec((tm, D), lambda i, j: (i, 0)),
            pl.BlockSpec((1, D), lambda i, j: (0, 0)),
            pl.BlockSpec((D, tn), lambda i, j: (0, j)),
            pl.BlockSpec((D, LANE), lambda i, j: (0, 0)),
            pl.BlockSpec((LANE, KD), lambda i, j: (0, 0)),
            pl.BlockSpec((1, KD), lambda i, j: (0, 0)),
        ],
        out_specs=[
            pl.BlockSpec((tm, tn), lambda i, j: (i, j)),
            pl.BlockSpec((tm, KD), lambda i, j: (i, 0)),
        ],
        out_shape=[jax.ShapeDtypeStruct((S, N), F32),
                   jax.ShapeDtypeStruct((S, KD), F32)],
        scratch_shapes=[pltpu.VMEM((tm, D), BF16)],
        compiler_params=_cparams("arbitrary", "arbitrary"),
        name="gla_in",
    )(x, norm_w, w_in, w1p, w2p, gk_b)


def _gla_core_kernel(q_ref, k_ref, v_ref, g_ref, la_ref, onw_ref, o_ref, st_ref, *, n_chunks):
    C = GLA_CHUNK
    H, dv, dk = st_ref.shape

    @pl.when(pl.program_id(0) == 0)
    def _():
        st_ref[...] = jnp.zeros_like(st_ref)

    row = lax.broadcasted_iota(jnp.int32, (C, C), 0)
    col = lax.broadcasted_iota(jnp.int32, (C, C), 1)
    causal = row >= col
    tril = causal.astype(BF16)
    q_scale = dk ** -0.5

    def chunk(c, carry):
        rows = pl.ds(pl.multiple_of(c * C, C), C)
        for h in range(H):
            kc = slice(h * dk, (h + 1) * dk)
            vc = slice(h * dv, (h + 1) * dv)
            la = la_ref[rows, kc]
            la_hi = la.astype(BF16)
            la_lo = (la - la_hi.astype(F32)).astype(BF16)
            b = _dot(tril, la_hi) + _dot(tril, la_lo)
            b_last = b[C - 1:C, :]
            q = q_ref[rows, kc]
            k = k_ref[rows, kc]
            v = v_ref[rows, vc].astype(BF16)
            q_dec = ((q * q_scale) * jnp.exp(b)).astype(BF16)
            k_inv = (k * jnp.exp(-b)).astype(BF16)
            k_end = (k * jnp.exp(b_last - b)).astype(BF16)
            chunk_decay = jnp.exp(b_last)
            attn = jnp.where(causal, _dot_nt(q_dec, k_inv), 0.0)
            st = st_ref[h]
            o = _dot(attn.astype(BF16), v) + _dot_nt(q_dec, st.astype(BF16))
            st_ref[h] = st * chunk_decay + _dot_tn(v, k_end)
            y = _rmsnorm_rows(o, onw_ref[...])
            g = g_ref[rows, vc]
            y = y * (g * (1.0 / (1.0 + jnp.exp(-g))))
            o_ref[rows, vc] = y.astype(BF16)
        return carry

    lax.fori_loop(0, n_chunks, chunk, 0)


def _gla_core(proj, log_a, o_norm_w, *, tile=256):
    S = proj.shape[0]
    H = GLA_HEADS
    KD = log_a.shape[1]
    VD = (proj.shape[1] - 2 * KD) // 2
    assert VD == 2 * KD
    return pl.pallas_call(
        functools.partial(_gla_core_kernel, n_chunks=tile // GLA_CHUNK),
        grid=(S // tile,),
        in_specs=[
            pl.BlockSpec((tile, KD), lambda t: (t, 0)),
            pl.BlockSpec((tile, KD), lambda t: (t, 1)),
            pl.BlockSpec((tile, VD), lambda t: (t, 1)),
            pl.BlockSpec((tile, VD), lambda t: (t, 2)),
            pl.BlockSpec((tile, KD), lambda t: (t, 0)),
            pl.BlockSpec((1, VD // H), lambda t: (0, 0)),
        ],
        out_specs=pl.BlockSpec((tile, VD), lambda t: (t, 0)),
        out_shape=jax.ShapeDtypeStruct((S, VD), BF16),
        scratch_shapes=[pltpu.VMEM((H, VD // H, KD // H), F32)],
        compiler_params=_cparams("arbitrary"),
        name="gla_core",
    )(proj, proj, proj, proj, log_a, o_norm_w)


def _mm_res_kernel(a_ref, w_ref, r_ref, o_ref):
    o_ref[...] = r_ref[...] + _dot(a_ref[...], w_ref[...])


def _mm_res(a, w, res, *, tm=1024, tn=1024):
    S, K = a.shape
    N = w.shape[1]
    return pl.pallas_call(
        _mm_res_kernel,
        grid=(S // tm, N // tn),
        in_specs=[
            pl.BlockSpec((tm, K), lambda i, j: (i, 0)),
            pl.BlockSpec((K, tn), lambda i, j: (0, j)),
            pl.BlockSpec((tm, tn), lambda i, j: (i, j)),
        ],
        out_specs=pl.BlockSpec((tm, tn), lambda i, j: (i, j)),
        out_shape=jax.ShapeDtypeStruct((S, N), F32),
        compiler_params=_cparams("arbitrary", "arbitrary"),
        name="mm_res",
    )(a, w, res)


def _ffn_kernel(xa_ref, xc_ref, nw_ref, wa_ref, wu_ref, cwa_ref, cwu_ref, cba_ref, cbu_ref, wd_ref,
                o_ref, xn_scr, halo_a, halo_u, ha0, hu0, ha1, hu1, act0, act1, *, n_chunks, n_steps):
    s = pl.program_id(0)
    last = n_steps - 1
    ja = jnp.minimum(s, last) % n_chunks
    jb = jnp.clip(s - 1, 0, last) % n_chunks
    jc = jnp.clip(s - 2, 0, last) % n_chunks

    @pl.when(s == 0)
    def _():
        for ref in (halo_a, halo_u, ha0, hu0, ha1, hu1, act0, act1):
            ref[...] = jnp.zeros(ref.shape, ref.dtype)

    @pl.when((ja == 0) & (s <= last))
    def _():
        xn_scr[...] = _rmsnorm_rows(xa_ref[...], nw_ref[...]).astype(BF16)

    @pl.when(jc == 0)
    def _():
        o_ref[...] = xc_ref[...]

    tm, tf = act0.shape
    top = HALO_ROWS

    n = FFN_ROW_STRIP
    sub = HALO_ROWS

    def taps(cw_ref, cb_ref, cols):
        rep = lambda r: jnp.broadcast_to(r, (sub, LANE))
        return (rep(cw_ref[0:1, cols]), rep(cw_ref[1:2, cols]), rep(cw_ref[2:3, cols]),
                rep(cb_ref[:, cols]))

    def conv_rows(h_ref, w, r0, cols):
        win = lambda off: h_ref[top + off + r0:top + off + r0 + n, cols].reshape(n // sub, sub, LANE)
        return w[3] + w[0] * win(-2) + w[1] * win(-1) + w[2] * win(0)

    def stages(h_out, h_in, act_out, act_in):
        for h_ref, halo_ref in ((h_in[0], halo_a), (h_in[1], halo_u)):
            h_ref[0:top, :] = halo_ref[jb]
            halo_ref[jb] = h_ref[tm:tm + top, :]
        for c0 in range(0, tf, LANE):
            cols = slice(c0, c0 + LANE)
            wa_t = taps(cwa_ref, cba_ref, cols)
            wu_t = taps(cwu_ref, cbu_ref, cols)
            for r0 in range(0, tm, n):
                ca = conv_rows(h_in[0], wa_t, r0, cols)
                cu = conv_rows(h_in[1], wu_t, r0, cols)
                act = (ca * (1.0 / (1.0 + jnp.exp(-ca)))) * cu
                act_out[r0:r0 + n, cols] = act.reshape(n, LANE).astype(BF16)
        xn = xn_scr[...]
        h_out[0][top:top + tm, :] = _dot(xn, wa_ref[...])
        h_out[1][top:top + tm, :] = _dot(xn, wu_ref[...])
        o_ref[...] += _dot(act_in[...], wd_ref[...])

    @pl.when(s % 2 == 0)
    def _():
        stages((ha0, hu0), (ha1, hu1), act0, act1)

    @pl.when(s % 2 == 1)
    def _():
        stages((ha1, hu1), (ha0, hu0), act1, act0)


def _conv_ffn(x, norm_w, w_up, conv_w, conv_b, w_down, *, tm=512, tf=512):
    S, D = x.shape
    Fh = w_down.shape[0]
    nF = Fh // tf
    T = (S // tm) * nF
    last = T - 1

    def a_step(s):
        return jnp.minimum(s, last)

    def b_chunk(s):
        return jnp.clip(s - 1, 0, last) % nF

    def c_step(s):
        return jnp.clip(s - 2, 0, last)

    return pl.pallas_call(
        functools.partial(_ffn_kernel, n_chunks=nF, n_steps=T),
        grid=(T + 2,),
        in_specs=[
            pl.BlockSpec((tm, D), lambda s: (a_step(s) // nF, 0)),
            pl.BlockSpec((tm, D), lambda s: (c_step(s) // nF, 0)),
            pl.BlockSpec((1, D), lambda s: (0, 0)),
            pl.BlockSpec((D, tf), lambda s: (0, a_step(s) % nF)),
            pl.BlockSpec((D, tf), lambda s: (0, nF + a_step(s) % nF)),
            pl.BlockSpec((CONV_WIDTH, tf), lambda s: (0, b_chunk(s))),
            pl.BlockSpec((CONV_WIDTH, tf), lambda s: (0, nF + b_chunk(s))),
            pl.BlockSpec((1, tf), lambda s: (0, b_chunk(s))),
            pl.BlockSpec((1, tf), lambda s: (0, nF + b_chunk(s))),
            pl.BlockSpec((tf, D), lambda s: (c_step(s) % nF, 0)),
        ],
        out_specs=pl.BlockSpec((tm, D), lambda s: (c_step(s) // nF, 0)),
        out_shape=jax.ShapeDtypeStruct((S, D), F32),
        scratch_shapes=[pltpu.VMEM((tm, D), BF16),
                        pltpu.VMEM((nF, HALO_ROWS, tf), F32),
                        pltpu.VMEM((nF, HALO_ROWS, tf), F32),
                        pltpu.VMEM((HALO_ROWS + tm, tf), F32), pltpu.VMEM((HALO_ROWS + tm, tf), F32),
                        pltpu.VMEM((HALO_ROWS + tm, tf), F32), pltpu.VMEM((HALO_ROWS + tm, tf), F32),
                        pltpu.VMEM((tm, tf), BF16), pltpu.VMEM((tm, tf), BF16)],
        compiler_params=_cparams("arbitrary"),
        name="conv_ffn",
    )(x, x, norm_w, w_up, w_up, conv_w, conv_w, conv_b, conv_b, w_down)


def _proj_headnorm_kernel(x_ref, nw_ref, w_ref, hw_ref, o_ref, xn_scr, *, scale):
    @pl.when(pl.program_id(1) == 0)
    def _():
        xn_scr[...] = _rmsnorm_rows(x_ref[...], nw_ref[...]).astype(BF16)

    y = _dot(xn_scr[...], w_ref[...])
    Dh = hw_ref.shape[1]
    for hh in range(y.shape[1] // Dh):
        cols = slice(hh * Dh, (hh + 1) * Dh)
        yn = _rmsnorm_rows(y[:, cols], hw_ref[...])
        if scale is not None:
            yn = yn * scale
        o_ref[:, cols] = yn.astype(BF16)


def _proj_headnorm(x, norm_w, w, head_w, n_out, *, scale=None, tm=1024, tn=512):
    S, D = x.shape
    Dh = head_w.shape[1]
    return pl.pallas_call(
        functools.partial(_proj_headnorm_kernel, scale=scale),
        grid=(S // tm, n_out // tn),
        in_specs=[
            pl.BlockSpec((tm, D), lambda i, j: (i, 0)),
            pl.BlockSpec((1, D), lambda i, j: (0, 0)),
            pl.BlockSpec((D, tn), lambda i, j: (0, j)),
            pl.BlockSpec((1, Dh), lambda i, j: (0, 0)),
        ],
        out_specs=pl.BlockSpec((tm, tn), lambda i, j: (i, j)),
        out_shape=jax.ShapeDtypeStruct((S, n_out), BF16),
        scratch_shapes=[pltpu.VMEM((tm, D), BF16)],
        compiler_params=_cparams("arbitrary", "arbitrary"),
        name="proj_headnorm",
    )(x, norm_w, w, head_w)


def _vproj_kernel(x_ref, nw_ref, w_ref, vt_ref, xn_scr):
    @pl.when(pl.program_id(1) == 0)
    def _():
        xn_scr[...] = _rmsnorm_rows(x_ref[...], nw_ref[...]).astype(BF16)

    y = _dot(xn_scr[...], w_ref[...])
    n_h, n_b, rows, BLK = vt_ref.shape
    Dh = MOBA_HEAD_DIM
    ones = jnp.ones((rows - Dh, BLK), BF16)
    for hh in range(n_h):
        for bb in range(n_b):
            blk = y[bb * BLK:(bb + 1) * BLK, hh * Dh:(hh + 1) * Dh]
            vt_ref[hh, bb, 0:Dh, :] = blk.T.astype(BF16)
            vt_ref[hh, bb, Dh:rows, :] = ones


def _vproj(x, norm_w, w_kv, col0, *, tm=1024, tn=512):
    S, D = x.shape
    H, Dh, BLK = MOBA_HEADS, MOBA_HEAD_DIM, MOBA_BLOCK
    off = col0 // tn
    return pl.pallas_call(
        _vproj_kernel,
        grid=(S // tm, (H * Dh) // tn),
        in_specs=[
            pl.BlockSpec((tm, D), lambda i, j: (i, 0)),
            pl.BlockSpec((1, D), lambda i, j: (0, 0)),
            pl.BlockSpec((D, tn), lambda i, j: (0, off + j)),
        ],
        out_specs=pl.BlockSpec((tn // Dh, tm // BLK, VT_ROWS, BLK), lambda i, j: (j, i, 0, 0)),
        out_shape=jax.ShapeDtypeStruct((H, S // BLK, VT_ROWS, BLK), BF16),
        scratch_shapes=[pltpu.VMEM((tm, D), BF16)],
        compiler_params=_cparams("arbitrary", "arbitrary"),
        name="v_proj",
    )(x, norm_w, w_kv)


def _kmean_kernel(k_ref, o_ref):
    rows, cols = k_ref.shape
    k = k_ref[...].astype(F32).reshape(rows // MOBA_BLOCK, MOBA_BLOCK, cols)
    o_ref[...] = jnp.sum(k, axis=1) * (1.0 / MOBA_BLOCK)


def _kmean(k, *, tr=2048, tc=512):
    S, N = k.shape
    return pl.pallas_call(
        _kmean_kernel,
        grid=(S // tr, N // tc),
        in_specs=[pl.BlockSpec((tr, tc), lambda i, j: (i, j))],
        out_specs=pl.BlockSpec((tr // MOBA_BLOCK, tc), lambda i, j: (i, j)),
        out_shape=jax.ShapeDtypeStruct((S // MOBA_BLOCK, N), F32),
        compiler_params=_cparams("arbitrary", "arbitrary"),
        name="kmean",
    )(k)


def _gate_kernel(q_ref, km_ref, mask_ref):
    nb = km_ref.shape[0]
    tq = q_ref.shape[0]
    g = _dot_nt(km_ref[...].astype(BF16), q_ref[...])
    blk = lax.broadcasted_iota(jnp.int32, (nb, tq), 0)
    pos = pl.program_id(1) * tq + lax.broadcasted_iota(jnp.int32, (nb, tq), 1)
    own = lax.shift_right_logical(pos, int(math.log2(MOBA_BLOCK)))
    neg = -jnp.inf
    v = jnp.where(blk < own, g, neg)
    sel = jnp.zeros((nb, tq), jnp.bool_)
    for r in range(min(MOBA_TOPK, nb)):
        m = jnp.max(v, axis=0, keepdims=True)
        first = jnp.min(jnp.where(v == m, blk, nb), axis=0, keepdims=True)
        onehot = blk == first
        sel = sel | (onehot & (own > r))
        v = jnp.where(onehot, neg, v)
    mask_ref[0] = jnp.where(sel, 0.0, neg)


def _gate_mask(q, kmean, *, tq=2048):
    S = q.shape[0]
    H, Dh = MOBA_HEADS, MOBA_HEAD_DIM
    nb = kmean.shape[0]
    return pl.pallas_call(
        _gate_kernel,
        grid=(H, S // tq),
        in_specs=[
            pl.BlockSpec((tq, Dh), lambda h, t: (t, h)),
            pl.BlockSpec((nb, Dh), lambda h, t: (0, h)),
        ],
        out_specs=pl.BlockSpec((1, nb, tq), lambda h, t: (h, 0, t)),
        out_shape=jax.ShapeDtypeStruct((H, nb, S), F32),
        compiler_params=_cparams("arbitrary", "arbitrary"),
        name="gate_topk",
    )(q, kmean)


def _rel_bias_tile(tbl_ref, h, dist):
    max_exact = REL_BUCKETS // 2
    n = jnp.maximum(dist, 0)
    large = max_exact + (jnp.log(jnp.maximum(n, 1).astype(F32) / max_exact)
                         / math.log(REL_MAX_DISTANCE / max_exact)
                         * (REL_BUCKETS - max_exact)).astype(jnp.int32)
    large = jnp.minimum(large, REL_BUCKETS - 1)
    bucket = jnp.where(n < max_exact, n, large)
    out = jnp.zeros(dist.shape, F32)
    for b in range(REL_BUCKETS):
        out = jnp.where(bucket == b, tbl_ref[h, b], out)
    return out


def _moba_kernel(tbl_ref, q_ref, k_ref, vt_ref, mask_ref, o_ref,
                 bown_ref, bprev_ref, sa_ref, sb_ref):
    BLK, G, Dh = MOBA_BLOCK, MOBA_KV_GROUP, MOBA_HEAD_DIM
    NH, nb = vt_ref.shape[0], vt_ref.shape[1]
    STEP = 2 * G
    head0 = pl.program_id(0) * NH
    i = pl.program_id(1)
    neg = -jnp.inf

    @pl.when(i == 0)
    def _():
        key = lax.broadcasted_iota(jnp.int32, (BLK, BLK), 0)
        qry = lax.broadcasted_iota(jnp.int32, (BLK, BLK), 1)
        dist = qry - key
        for hh in range(NH):
            own = _rel_bias_tile(tbl_ref, head0 + hh, dist) * LOG2E
            bown_ref[hh] = jnp.where(dist >= 0, own, neg)
            bprev_ref[hh] = _rel_bias_tile(tbl_ref, head0 + hh, dist + BLK) * LOG2E

    q = [q_ref[:, hh * Dh:(hh + 1) * Dh] for hh in range(NH)]

    def k_rows(hh, j, n):
        return k_ref[pl.ds(pl.multiple_of(j * BLK, BLK), n * BLK), hh * Dh:(hh + 1) * Dh]

    def colmax(s):
        return jnp.max(s, axis=0, keepdims=True)

    def put_scores(s_ref, j0):
        for hh in range(NH):
            s_ref[hh] = _dot_nt(k_rows(hh, j0, G), q[hh])

    put_scores(sa_ref, 0)

    jp = jnp.maximum(i - 1, 0)
    carry = []
    for hh in range(NH):
        s_own = _dot_nt(k_rows(hh, i, 1), q[hh]) + bown_ref[hh]
        s_prev = _dot_nt(k_rows(hh, jp, 1), q[hh]) + bprev_ref[hh]
        sel_prev = mask_ref[hh, pl.ds(jp, 1), :]
        m = jnp.maximum(colmax(s_own), colmax(s_prev) + sel_prev)
        p_own = jnp.exp2(s_own - m).astype(BF16)
        p_prev = jnp.exp2(s_prev - (m - sel_prev)).astype(BF16)
        carry += [m, _dot(vt_ref[hh, i], p_own) + _dot(vt_ref[hh, jp], p_prev)]

    far_bias = [tbl_ref[head0 + hh, REL_BUCKETS - 1] * LOG2E for hh in range(NH)]

    def update(s_ref, j0, carry):
        out = []
        for hh in range(NH):
            m, acc = carry[2 * hh], carry[2 * hh + 1]
            sel = []
            m_new = m
            for g in range(G):
                row = (mask_ref[hh, pl.ds(j0 + g, 1), :]
                       + jnp.where(j0 + g < i - 1, far_bias[hh], neg))
                sel.append(row)
                m_new = jnp.maximum(m_new, colmax(s_ref[hh, g * BLK:(g + 1) * BLK, :]) + row)
            acc = acc * jnp.exp2(m - m_new)
            for g in range(G):
                p = jnp.exp2(s_ref[hh, g * BLK:(g + 1) * BLK, :] - (m_new - sel[g])).astype(BF16)
                acc = acc + _dot(vt_ref[hh, j0 + g], p)
            out += [m_new, acc]
        return out

    def trip(t, carry):
        j0 = t * STEP
        put_scores(sb_ref, j0 + G)
        carry = update(sa_ref, j0, list(carry))
        put_scores(sa_ref, jnp.minimum(j0 + STEP, nb - G))
        carry = update(sb_ref, j0 + G, carry)
        return tuple(carry)

    n_trips = (i + (STEP - 2)) // STEP
    carry = lax.fori_loop(0, n_trips, trip, tuple(carry))
    for hh in range(NH):
        acc = carry[2 * hh + 1]
        o_ref[:, hh * Dh:(hh + 1) * Dh] = (acc[:Dh] / acc[Dh:Dh + 1]).T.astype(BF16)


def _moba_attn(tbl, q, k, vt, mask):
    S = q.shape[0]
    H, Dh, BLK = MOBA_HEADS, MOBA_HEAD_DIM, MOBA_BLOCK
    nb = S // BLK
    NH, G = MOBA_HEADS_PER_STEP, MOBA_KV_GROUP
    return pl.pallas_call(
        _moba_kernel,
        grid=(H // NH, nb),
        in_specs=[
            pl.BlockSpec(memory_space=pltpu.SMEM),
            pl.BlockSpec((BLK, NH * Dh), lambda h, i: (i, h)),
            pl.BlockSpec((S, NH * Dh), lambda h, i: (0, h)),
            pl.BlockSpec((NH, nb, VT_ROWS, BLK), lambda h, i: (h, 0, 0, 0)),
            pl.BlockSpec((NH, nb, BLK), lambda h, i: (h, 0, i)),
        ],
        out_specs=pl.BlockSpec((BLK, NH * Dh), lambda h, i: (i, h)),
        out_shape=jax.ShapeDtypeStruct((S, H * Dh), BF16),
        scratch_shapes=[pltpu.VMEM((NH, BLK, BLK), F32), pltpu.VMEM((NH, BLK, BLK), F32),
                        pltpu.VMEM((NH, G * BLK, BLK), F32), pltpu.VMEM((NH, G * BLK, BLK), F32)],
        compiler_params=_cparams("arbitrary", "arbitrary"),
        name="moba_attn",
    )(tbl, q, k, vt, mask)


def _row(v):
    return v.reshape(1, -1)


def kernel(x, gla_norm, gla_w_in, gla_gk_w1, gla_gk_w2, gla_gk_b, gla_o_norm, gla_w_out,
           kv_norm, kv_w, k_norm_w, moba_norm, moba_w_q, moba_q_norm, moba_w_out, rel_bias,
           ffn_norm, ffn_w_up, ffn_conv_w, ffn_conv_b, ffn_w_down):
    B, S, D = x.shape
    assert B == 1
    depth = ffn_norm.shape[0]
    n_a = gla_norm.shape[0]
    h = x[0]
    HD = MOBA_HEADS * MOBA_HEAD_DIM
    k_bf = vt = kmean = None
    tbl = rel_bias.T

    for layer in range(depth):
        if layer < n_a:
            a = layer
            R = gla_gk_w1.shape[2]
            w1p = jnp.pad(gla_gk_w1[a].astype(BF16), ((0, 0), (0, LANE - R)))
            w2p = jnp.pad(gla_gk_w2[a].astype(BF16), ((0, LANE - R), (0, 0)))
            proj, log_a = _gla_in(h, _row(gla_norm[a]), gla_w_in[a].astype(BF16), w1p, w2p,
                                  _row(gla_gk_b[a]))
            o = _gla_core(proj, log_a, _row(gla_o_norm[a]))
            h = _mm_res(o, gla_w_out[a].astype(BF16), h)
        else:
            b = layer - n_a
            if k_bf is None:
                kv_bf = kv_w.astype(BF16)
                k_bf = _proj_headnorm(h, _row(kv_norm), kv_bf, _row(k_norm_w), HD)
                vt = _vproj(h, _row(kv_norm), kv_bf, HD)
                kmean = _kmean(k_bf)
            q = _proj_headnorm(h, _row(moba_norm[b]), moba_w_q[b].astype(BF16),
                               _row(moba_q_norm[b]), HD,
                               scale=MOBA_HEAD_DIM ** -0.5 * LOG2E)
            mask = _gate_mask(q, kmean)
            o = _moba_attn(tbl, q, k_bf, vt, mask)
            h = _mm_res(o, moba_w_out[b].astype(BF16), h)
        h = _conv_ffn(h, _row(ffn_norm[layer]), ffn_w_up[layer].astype(BF16),
                      ffn_conv_w[layer], _row(ffn_conv_b[layer]), ffn_w_down[layer].astype(BF16))
    return h[None]
```

```python
import functools
import math

import jax
import jax.numpy as jnp
from jax import lax
from jax.experimental import pallas as pl
from jax.experimental.pallas import tpu as pltpu

F32 = jnp.float32
BF16 = jnp.bfloat16

EPS = 1e-6
GLA_HEADS = 4
GLA_CHUNK = 64
GLA_GATE_NORMALIZER = 16.0
MOBA_HEADS = 16
MOBA_HEAD_DIM = 128
MOBA_BLOCK = 256
MOBA_TOPK = 3
REL_BUCKETS = 32
REL_MAX_DISTANCE = 128
CONV_WIDTH = 3

VMEM_LIMIT_BYTES = 56 * 1024 * 1024
LANE = 128
HALO_ROWS = 8
BF16_SUBLANES = 16
MOBA_KV_GROUP = 2
MOBA_HEADS_PER_STEP = 2
LOG2E = math.log2(math.e)
FFN_ROW_STRIP = 64
VT_ROWS = MOBA_HEAD_DIM + BF16_SUBLANES


def _cparams(*sem):
    return pltpu.CompilerParams(dimension_semantics=sem,
                                vmem_limit_bytes=VMEM_LIMIT_BYTES)


def _rmsnorm_rows(x, w):
    ms = jnp.mean(x * x, axis=-1, keepdims=True)
    return x * lax.rsqrt(ms + EPS) * w


def _dot(a, b):
    return jnp.dot(a, b, preferred_element_type=F32)


def _dot_nt(a, b):
    return lax.dot_general(a, b, (((1,), (1,)), ((), ())), preferred_element_type=F32)


def _dot_tn(a, b):
    return lax.dot_general(a, b, (((0,), (0,)), ((), ())), preferred_element_type=F32)


def _gla_in_kernel(x_ref, nw_ref, w_ref, w1_ref, w2_ref, b_ref, proj_ref, la_ref, xn_scr):
    @pl.when(pl.program_id(1) == 0)
    def _():
        xn = _rmsnorm_rows(x_ref[...], nw_ref[...]).astype(BF16)
        xn_scr[...] = xn
        r = _dot(xn, w1_ref[...])
        gk = _dot(r.astype(BF16), w2_ref[...]) + b_ref[...]
        log_sig = jnp.minimum(gk, 0.0) - jnp.log1p(jnp.exp(-jnp.abs(gk)))
        la_ref[...] = log_sig * (1.0 / GLA_GATE_NORMALIZER)

    proj_ref[...] = _dot(xn_scr[...], w_ref[...])


def _chunk_major(w, tn):
    K, N = w.shape
    return w.reshape(K, N // tn, tn).transpose(1, 0, 2)


def _gla_in(x, norm_w, w_in, w1p, w2p, gk_b, *, tm=1024):
    S, D = x.shape
    N = w_in.shape[1]
    KD = w2p.shape[1]
    tn = KD
    return pl.pallas_call(
        _gla_in_kernel,
        grid=(S // tm, N // tn),
        in_specs=[
            pl.BlockSpec((tm, D), lambda i, j: (i, 0)),
            pl.BlockSpec((1, D), lambda i, j: (0, 0)),
            pl.BlockSpec((None, D, tn), lambda i, j: (j, 0, 0)),
            pl.BlockSpec((D, LANE), lambda i, j: (0, 0)),
            pl.BlockSpec((LANE, KD), lambda i, j: (0, 0)),
            pl.BlockSpec((1, KD), lambda i, j: (0, 0)),
        ],
        out_specs=[
            pl.BlockSpec((None, tm, tn), lambda i, j: (j, i, 0)),
            pl.BlockSpec((tm, KD), lambda i, j: (i, 0)),
        ],
        out_shape=[jax.ShapeDtypeStruct((N // tn, S, tn), F32),
                   jax.ShapeDtypeStruct((S, KD), F32)],
        scratch_shapes=[pltpu.VMEM((tm, D), BF16)],
        compiler_params=_cparams("arbitrary", "arbitrary"),
        name="gla_in",
    )(x, norm_w, _chunk_major(w_in, tn), w1p, w2p, gk_b)


def _gla_core_kernel(q_ref, k_ref, v_ref, g_ref, la_ref, onw_ref, o_ref, st_ref, *, n_chunks):
    C = GLA_CHUNK
    H, dv, dk = st_ref.shape

    @pl.when(pl.program_id(0) == 0)
    def _():
        st_ref[...] = jnp.zeros_like(st_ref)

    row = lax.broadcasted_iota(jnp.int32, (C, C), 0)
    col = lax.broadcasted_iota(jnp.int32, (C, C), 1)
    causal = row >= col
    tril = causal.astype(BF16)
    q_scale = dk ** -0.5

    def chunk(c, carry):
        rows = pl.ds(pl.multiple_of(c * C, C), C)
        for h in range(H):
            kc = slice(h * dk, (h + 1) * dk)
            vc = slice(h * dv, (h + 1) * dv)
            per = v_ref.shape[2] // dv
            vchunk, vcc = h // per, slice((h % per) * dv, (h % per + 1) * dv)
            la = la_ref[rows, kc]
            la_hi = la.astype(BF16)
            la_lo = (la - la_hi.astype(F32)).astype(BF16)
            b = _dot(tril, la_hi) + _dot(tril, la_lo)
            b_last = b[C - 1:C, :]
            q = q_ref[rows, kc]
            k = k_ref[rows, kc]
            v = v_ref[vchunk, rows, vcc].astype(BF16)
            q_dec = ((q * q_scale) * jnp.exp(b)).astype(BF16)
            k_inv = (k * jnp.exp(-b)).astype(BF16)
            k_end = (k * jnp.exp(b_last - b)).astype(BF16)
            chunk_decay = jnp.exp(b_last)
            attn = jnp.where(causal, _dot_nt(q_dec, k_inv), 0.0)
            st = st_ref[h]
            o = _dot(attn.astype(BF16), v) + _dot_nt(q_dec, st.astype(BF16))
            st_ref[h] = st * chunk_decay + _dot_tn(v, k_end)
            y = _rmsnorm_rows(o, onw_ref[...])
            g = g_ref[vchunk, rows, vcc]
            y = y * (g * (1.0 / (1.0 + jnp.exp(-g))))
            o_ref[rows, vc] = y.astype(BF16)
        return carry

    lax.fori_loop(0, n_chunks, chunk, 0)


def _gla_core(proj, log_a, o_norm_w, *, tile=256):
    n_col_chunks, S, KD = proj.shape
    H = GLA_HEADS
    assert log_a.shape[1] == KD
    nv = (n_col_chunks - 2) // 2
    assert 2 % nv == 0 and n_col_chunks == 2 + 2 * nv
    VD = nv * KD
    return pl.pallas_call(
        functools.partial(_gla_core_kernel, n_chunks=tile // GLA_CHUNK),
        grid=(S // tile,),
        in_specs=[
            pl.BlockSpec((None, tile, KD), lambda t: (0, t, 0)),
            pl.BlockSpec((None, tile, KD), lambda t: (1, t, 0)),
            pl.BlockSpec((nv, tile, KD), lambda t: (2 // nv, t, 0)),
            pl.BlockSpec((nv, tile, KD), lambda t: ((2 + nv) // nv, t, 0)),
            pl.BlockSpec((tile, KD), lambda t: (t, 0)),
            pl.BlockSpec((1, VD // H), lambda t: (0, 0)),
        ],
        out_specs=pl.BlockSpec((tile, VD), lambda t: (t, 0)),
        out_shape=jax.ShapeDtypeStruct((S, VD), BF16),
        scratch_shapes=[pltpu.VMEM((H, VD // H, KD // H), F32)],
        compiler_params=_cparams("arbitrary"),
        name="gla_core",
    )(proj, proj, proj, proj, log_a, o_norm_w)


def _mm_res_kernel(a_ref, w_ref, r_ref, o_ref):
    o_ref[...] = r_ref[...] + _dot(a_ref[...], w_ref[...])


def _mm_res(a, w, res, *, tm=512, tn=2048):
    S, K = a.shape
    N = w.shape[1]
    return pl.pallas_call(
        _mm_res_kernel,
        grid=(S // tm, N // tn),
        in_specs=[
            pl.BlockSpec((tm, K), lambda i, j: (i, 0)),
            pl.BlockSpec((K, tn), lambda i, j: (0, j)),
            pl.BlockSpec((tm, tn), lambda i, j: (i, j)),
        ],
        out_specs=pl.BlockSpec((tm, tn), lambda i, j: (i, j)),
        out_shape=jax.ShapeDtypeStruct((S, N), F32),
        compiler_params=_cparams("arbitrary", "arbitrary"),
        name="mm_res",
    )(a, w, res)


def _ffn_kernel(xa_ref, xc_ref, nw_ref, wa_ref, wu_ref, cwa_ref, cwu_ref, cba_ref, cbu_ref, wd_ref,
                o_ref, xn_scr, halo_a, halo_u, ha0, hu0, ha1, hu1, act0, act1, *, n_chunks, n_steps):
    s = pl.program_id(0)
    last = n_steps - 1
    ja = jnp.minimum(s, last) % n_chunks
    jb = jnp.clip(s - 1, 0, last) % n_chunks
    jc = jnp.clip(s - 2, 0, last) % n_chunks

    @pl.when(s == 0)
    def _():
        for ref in (halo_a, halo_u, ha0, hu0, ha1, hu1, act0, act1):
            ref[...] = jnp.zeros(ref.shape, ref.dtype)

    @pl.when((ja == 0) & (s <= last))
    def _():
        xn_scr[...] = _rmsnorm_rows(xa_ref[...], nw_ref[...]).astype(BF16)

    @pl.when(jc == 0)
    def _():
        o_ref[...] = xc_ref[...]

    tm, tf = act0.shape
    top = HALO_ROWS

    n = FFN_ROW_STRIP
    sub = HALO_ROWS

    def taps(cw_ref, cb_ref, cols):
        rep = lambda r: jnp.broadcast_to(r, (sub, LANE))
        return (rep(cw_ref[0:1, cols]), rep(cw_ref[1:2, cols]), rep(cw_ref[2:3, cols]),
                rep(cb_ref[:, cols]))

    def conv_rows(h_ref, w, r0, cols):
        win = lambda off: h_ref[top + off + r0:top + off + r0 + n, cols].reshape(n // sub, sub, LANE)
        return w[3] + w[0] * win(-2) + w[1] * win(-1) + w[2] * win(0)

    def stages(h_out, h_in, act_out, act_in):
        for h_ref, halo_ref in ((h_in[0], halo_a), (h_in[1], halo_u)):
            h_ref[0:top, :] = halo_ref[jb]
            halo_ref[jb] = h_ref[tm:tm + top, :]
        for c0 in range(0, tf, LANE):
            cols = slice(c0, c0 + LANE)
            wa_t = taps(cwa_ref, cba_ref, cols)
            wu_t = taps(cwu_ref, cbu_ref, cols)
            for r0 in range(0, tm, n):
                ca = conv_rows(h_in[0], wa_t, r0, cols)
                cu = conv_rows(h_in[1], wu_t, r0, cols)
                act = (ca * (1.0 / (1.0 + jnp.exp(-ca)))) * cu
                act_out[r0:r0 + n, cols] = act.reshape(n, LANE).astype(BF16)
        xn = xn_scr[...]
        h_out[0][top:top + tm, :] = _dot(xn, wa_ref[...])
        h_out[1][top:top + tm, :] = _dot(xn, wu_ref[...])
        o_ref[...] += _dot(act_in[...], wd_ref[...])

    @pl.when(s % 2 == 0)
    def _():
        stages((ha0, hu0), (ha1, hu1), act0, act1)

    @pl.when(s % 2 == 1)
    def _():
        stages((ha1, hu1), (ha0, hu0), act1, act0)


def _conv_ffn(x, norm_w, w_up, conv_w, conv_b, w_down, *, tm=512, tf=512):
    S, D = x.shape
    Fh = w_down.shape[0]
    nF = Fh // tf
    T = (S // tm) * nF
    last = T - 1
    w_up_cm = _chunk_major(w_up, tf)

    def a_step(s):
        return jnp.minimum(s, last)

    def b_chunk(s):
        return jnp.clip(s - 1, 0, last) % nF

    def c_step(s):
        return jnp.clip(s - 2, 0, last)

    return pl.pallas_call(
        functools.partial(_ffn_kernel, n_chunks=nF, n_steps=T),
        grid=(T + 2,),
        in_specs=[
            pl.BlockSpec((tm, D), lambda s: (a_step(s) // nF, 0)),
            pl.BlockSpec((tm, D), lambda s: (c_step(s) // nF, 0)),
            pl.BlockSpec((1, D), lambda s: (0, 0)),
            pl.BlockSpec((None, D, tf), lambda s: (a_step(s) % nF, 0, 0)),
            pl.BlockSpec((None, D, tf), lambda s: (nF + a_step(s) % nF, 0, 0)),
            pl.BlockSpec((CONV_WIDTH, tf), lambda s: (0, b_chunk(s))),
            pl.BlockSpec((CONV_WIDTH, tf), lambda s: (0, nF + b_chunk(s))),
            pl.BlockSpec((1, tf), lambda s: (0, b_chunk(s))),
            pl.BlockSpec((1, tf), lambda s: (0, nF + b_chunk(s))),
            pl.BlockSpec((tf, D), lambda s: (c_step(s) % nF, 0)),
        ],
        out_specs=pl.BlockSpec((tm, D), lambda s: (c_step(s) // nF, 0)),
        out_shape=jax.ShapeDtypeStruct((S, D), F32),
        scratch_shapes=[pltpu.VMEM((tm, D), BF16),
                        pltpu.VMEM((nF, HALO_ROWS, tf), F32),
                        pltpu.VMEM((nF, HALO_ROWS, tf), F32),
                        pltpu.VMEM((HALO_ROWS + tm, tf), F32), pltpu.VMEM((HALO_ROWS + tm, tf), F32),
                        pltpu.VMEM((HALO_ROWS + tm, tf), F32), pltpu.VMEM((HALO_ROWS + tm, tf), F32),
                        pltpu.VMEM((tm, tf), BF16), pltpu.VMEM((tm, tf), BF16)],
        compiler_params=_cparams("arbitrary"),
        name="conv_ffn",
    )(x, x, norm_w, w_up_cm, w_up_cm, conv_w, conv_w, conv_b, conv_b, w_down)


def _proj_headnorm_kernel(x_ref, nw_ref, w_ref, hw_ref, o_ref, xn_scr, *, scale):
    @pl.when(pl.program_id(1) == 0)
    def _():
        xn_scr[...] = _rmsnorm_rows(x_ref[...], nw_ref[...]).astype(BF16)

    y = _dot(xn_scr[...], w_ref[...])
    Dh = hw_ref.shape[1]
    for hh in range(y.shape[1] // Dh):
        cols = slice(hh * Dh, (hh + 1) * Dh)
        yn = _rmsnorm_rows(y[:, cols], hw_ref[...])
        if scale is not None:
            yn = yn * scale
        o_ref[:, cols] = yn.astype(BF16)


def _proj_headnorm(x, norm_w, w, head_w, n_out, *, scale=None, tm=512, tn=2048):
    S, D = x.shape
    Dh = head_w.shape[1]
    return pl.pallas_call(
        functools.partial(_proj_headnorm_kernel, scale=scale),
        grid=(S // tm, n_out // tn),
        in_specs=[
            pl.BlockSpec((tm, D), lambda i, j: (i, 0)),
            pl.BlockSpec((1, D), lambda i, j: (0, 0)),
            pl.BlockSpec((D, tn), lambda i, j: (0, j)),
            pl.BlockSpec((1, Dh), lambda i, j: (0, 0)),
        ],
        out_specs=pl.BlockSpec((tm, tn), lambda i, j: (i, j)),
        out_shape=jax.ShapeDtypeStruct((S, n_out), BF16),
        scratch_shapes=[pltpu.VMEM((tm, D), BF16)],
        compiler_params=_cparams("arbitrary", "arbitrary"),
        name="proj_headnorm",
    )(x, norm_w, w, head_w)


def _vproj_kernel(x_ref, nw_ref, w_ref, vt_ref, xn_scr):
    @pl.when(pl.program_id(1) == 0)
    def _():
        xn_scr[...] = _rmsnorm_rows(x_ref[...], nw_ref[...]).astype(BF16)

    y = _dot(xn_scr[...], w_ref[...])
    n_h, n_b, rows, BLK = vt_ref.shape
    Dh = MOBA_HEAD_DIM
    ones = jnp.ones((rows - Dh, BLK), BF16)
    for hh in range(n_h):
        for bb in range(n_b):
            blk = y[bb * BLK:(bb + 1) * BLK, hh * Dh:(hh + 1) * Dh]
            vt_ref[hh, bb, 0:Dh, :] = blk.T.astype(BF16)
            vt_ref[hh, bb, Dh:rows, :] = ones


def _vproj(x, norm_w, w_kv, col0, *, tm=512, tn=2048):
    S, D = x.shape
    H, Dh, BLK = MOBA_HEADS, MOBA_HEAD_DIM, MOBA_BLOCK
    off = col0 // tn
    return pl.pallas_call(
        _vproj_kernel,
        grid=(S // tm, (H * Dh) // tn),
        in_specs=[
            pl.BlockSpec((tm, D), lambda i, j: (i, 0)),
            pl.BlockSpec((1, D), lambda i, j: (0, 0)),
            pl.BlockSpec((D, tn), lambda i, j: (0, off + j)),
        ],
        out_specs=pl.BlockSpec((tn // Dh, tm // BLK, VT_ROWS, BLK), lambda i, j: (j, i, 0, 0)),
        out_shape=jax.ShapeDtypeStruct((H, S // BLK, VT_ROWS, BLK), BF16),
        scratch_shapes=[pltpu.VMEM((tm, D), BF16)],
        compiler_params=_cparams("arbitrary", "arbitrary"),
        name="v_proj",
    )(x, norm_w, w_kv)


def _kmean_kernel(k_ref, o_ref):
    rows, cols = k_ref.shape
    k = k_ref[...].astype(F32).reshape(rows // MOBA_BLOCK, MOBA_BLOCK, cols)
    o_ref[...] = jnp.sum(k, axis=1) * (1.0 / MOBA_BLOCK)


def _kmean(k, *, tr=2048, tc=512):
    S, N = k.shape
    return pl.pallas_call(
        _kmean_kernel,
        grid=(S // tr, N // tc),
        in_specs=[pl.BlockSpec((tr, tc), lambda i, j: (i, j))],
        out_specs=pl.BlockSpec((tr // MOBA_BLOCK, tc), lambda i, j: (i, j)),
        out_shape=jax.ShapeDtypeStruct((S // MOBA_BLOCK, N), F32),
        compiler_params=_cparams("arbitrary", "arbitrary"),
        name="kmean",
    )(k)


def _gate_kernel(q_ref, km_ref, mask_ref):
    nb = km_ref.shape[0]
    tq = q_ref.shape[0]
    g = _dot_nt(km_ref[...].astype(BF16), q_ref[...])
    blk = lax.broadcasted_iota(jnp.int32, (nb, tq), 0)
    pos = pl.program_id(1) * tq + lax.broadcasted_iota(jnp.int32, (nb, tq), 1)
    own = lax.shift_right_logical(pos, int(math.log2(MOBA_BLOCK)))
    neg = -jnp.inf
    v = jnp.where(blk < own, g, neg)
    sel = jnp.zeros((nb, tq), jnp.bool_)
    for r in range(min(MOBA_TOPK, nb)):
        m = jnp.max(v, axis=0, keepdims=True)
        first = jnp.min(jnp.where(v == m, blk, nb), axis=0, keepdims=True)
        onehot = blk == first
        sel = sel | (onehot & (own > r))
        v = jnp.where(onehot, neg, v)
    mask_ref[0] = jnp.where(sel, 0.0, neg)


def _gate_mask(q, kmean, *, tq=2048):
    S = q.shape[0]
    H, Dh = MOBA_HEADS, MOBA_HEAD_DIM
    nb = kmean.shape[0]
    return pl.pallas_call(
        _gate_kernel,
        grid=(H, S // tq),
        in_specs=[
            pl.BlockSpec((tq, Dh), lambda h, t: (t, h)),
            pl.BlockSpec((nb, Dh), lambda h, t: (0, h)),
        ],
        out_specs=pl.BlockSpec((1, nb, tq), lambda h, t: (h, 0, t)),
        out_shape=jax.ShapeDtypeStruct((H, nb, S), F32),
        compiler_params=_cparams("arbitrary", "arbitrary"),
        name="gate_topk",
    )(q, kmean)


def _rel_bias_tile(tbl_ref, h, dist):
    max_exact = REL_BUCKETS // 2
    n = jnp.maximum(dist, 0)
    large = max_exact + (jnp.log(jnp.maximum(n, 1).astype(F32) / max_exact)
                         / math.log(REL_MAX_DISTANCE / max_exact)
                         * (REL_BUCKETS - max_exact)).astype(jnp.int32)
    large = jnp.minimum(large, REL_BUCKETS - 1)
    bucket = jnp.where(n < max_exact, n, large)
    out = jnp.zeros(dist.shape, F32)
    for b in range(REL_BUCKETS):
        out = jnp.where(bucket == b, tbl_ref[h, b], out)
    return out


def _moba_kernel(tbl_ref, q_ref, k_ref, vt_ref, mask_ref, o_ref,
                 bown_ref, bprev_ref, sa_ref, sb_ref):
    BLK, G, Dh = MOBA_BLOCK, MOBA_KV_GROUP, MOBA_HEAD_DIM
    NH, nb = vt_ref.shape[0], vt_ref.shape[1]
    STEP = 2 * G
    head0 = pl.program_id(0) * NH
    i = pl.program_id(1)
    neg = -jnp.inf

    @pl.when(i == 0)
    def _():
        key = lax.broadcasted_iota(jnp.int32, (BLK, BLK), 0)
        qry = lax.broadcasted_iota(jnp.int32, (BLK, BLK), 1)
        dist = qry - key
        for hh in range(NH):
            own = _rel_bias_tile(tbl_ref, head0 + hh, dist) * LOG2E
            bown_ref[hh] = jnp.where(dist >= 0, own, neg)
            bprev_ref[hh] = _rel_bias_tile(tbl_ref, head0 + hh, dist + BLK) * LOG2E

    q = [q_ref[:, hh * Dh:(hh + 1) * Dh] for hh in range(NH)]

    def k_rows(hh, j, n):
        return k_ref[pl.ds(pl.multiple_of(j * BLK, BLK), n * BLK), hh * Dh:(hh + 1) * Dh]

    def colmax(s):
        return jnp.max(s, axis=0, keepdims=True)

    def put_scores(s_ref, j0):
        for hh in range(NH):
            s_ref[hh] = _dot_nt(k_rows(hh, j0, G), q[hh])

    put_scores(sa_ref, 0)

    jp = jnp.maximum(i - 1, 0)
    carry = []
    for hh in range(NH):
        s_own = _dot_nt(k_rows(hh, i, 1), q[hh]) + bown_ref[hh]
        s_prev = _dot_nt(k_rows(hh, jp, 1), q[hh]) + bprev_ref[hh]
        sel_prev = mask_ref[hh, pl.ds(jp, 1), :]
        m = jnp.maximum(colmax(s_own), colmax(s_prev) + sel_prev)
        p_own = jnp.exp2(s_own - m).astype(BF16)
        p_prev = jnp.exp2(s_prev - (m - sel_prev)).astype(BF16)
        carry += [m, _dot(vt_ref[hh, i], p_own) + _dot(vt_ref[hh, jp], p_prev)]

    far_bias = [tbl_ref[head0 + hh, REL_BUCKETS - 1] * LOG2E for hh in range(NH)]

    def update(s_ref, j0, carry):
        out = []
        for hh in range(NH):
            m, acc = carry[2 * hh], carry[2 * hh + 1]
            sel = []
            m_new = m
            for g in range(G):
                row = (mask_ref[hh, pl.ds(j0 + g, 1), :]
                       + jnp.where(j0 + g < i - 1, far_bias[hh], neg))
                sel.append(row)
                m_new = jnp.maximum(m_new, colmax(s_ref[hh, g * BLK:(g + 1) * BLK, :]) + row)
            acc = acc * jnp.exp2(m - m_new)
            for g in range(G):
                p = jnp.exp2(s_ref[hh, g * BLK:(g + 1) * BLK, :] - (m_new - sel[g])).astype(BF16)
                acc = acc + _dot(vt_ref[hh, j0 + g], p)
            out += [m_new, acc]
        return out

    def trip(t, carry):
        j0 = t * STEP
        put_scores(sb_ref, j0 + G)
        carry = update(sa_ref, j0, list(carry))
        put_scores(sa_ref, jnp.minimum(j0 + STEP, nb - G))
        carry = update(sb_ref, j0 + G, carry)
        return tuple(carry)

    n_trips = (i + (STEP - 2)) // STEP
    carry = lax.fori_loop(0, n_trips, trip, tuple(carry))
    for hh in range(NH):
        acc = carry[2 * hh + 1]
        o_ref[:, hh * Dh:(hh + 1) * Dh] = (acc[:Dh] / acc[Dh:Dh + 1]).T.astype(BF16)


def _moba_attn(tbl, q, k, vt, mask):
    S = q.shape[0]
    H, Dh, BLK = MOBA_HEADS, MOBA_HEAD_DIM, MOBA_BLOCK
    nb = S // BLK
    NH, G = MOBA_HEADS_PER_STEP, MOBA_KV_GROUP
    return pl.pallas_call(
        _moba_kernel,
        grid=(H // NH, nb),
        in_specs=[
            pl.BlockSpec(memory_space=pltpu.SMEM),
            pl.BlockSpec((BLK, NH * Dh), lambda h, i: (i, h)),
            pl.BlockSpec((S, NH * Dh), lambda h, i: (0, h)),
            pl.BlockSpec((NH, nb, VT_ROWS, BLK), lambda h, i: (h, 0, 0, 0)),
            pl.BlockSpec((NH, nb, BLK), lambda h, i: (h, 0, i)),
        ],
        out_specs=pl.BlockSpec((BLK, NH * Dh), lambda h, i: (i, h)),
        out_shape=jax.ShapeDtypeStruct((S, H * Dh), BF16),
        scratch_shapes=[pltpu.VMEM((NH, BLK, BLK), F32), pltpu.VMEM((NH, BLK, BLK), F32),
                        pltpu.VMEM((NH, G * BLK, BLK), F32), pltpu.VMEM((NH, G * BLK, BLK), F32)],
        compiler_params=_cparams("arbitrary", "arbitrary"),
        name="moba_attn",
    )(tbl, q, k, vt, mask)


def _row(v):
    return v.reshape(1, -1)


def kernel(x, gla_norm, gla_w_in, gla_gk_w1, gla_gk_w2, gla_gk_b, gla_o_norm, gla_w_out,
           kv_norm, kv_w, k_norm_w, moba_norm, moba_w_q, moba_q_norm, moba_w_out, rel_bias,
           ffn_norm, ffn_w_up, ffn_conv_w, ffn_conv_b, ffn_w_down):
    B, S, D = x.shape
    assert B == 1
    depth = ffn_norm.shape[0]
    n_a = gla_norm.shape[0]
    h = x[0]
    HD = MOBA_HEADS * MOBA_HEAD_DIM
    k_bf = vt = kmean = None
    tbl = rel_bias.T

    for layer in range(depth):
        if layer < n_a:
            a = layer
            R = gla_gk_w1.shape[2]
            w1p = jnp.pad(gla_gk_w1[a].astype(BF16), ((0, 0), (0, LANE - R)))
            w2p = jnp.pad(gla_gk_w2[a].astype(BF16), ((0, LANE - R), (0, 0)))
            proj, log_a = _gla_in(h, _row(gla_norm[a]), gla_w_in[a].astype(BF16), w1p, w2p,
                                  _row(gla_gk_b[a]))
            o = _gla_core(proj, log_a, _row(gla_o_norm[a]))
            h = _mm_res(o, gla_w_out[a].astype(BF16), h)
        else:
            b = layer - n_a
            if k_bf is None:
                kv_bf = kv_w.astype(BF16)
                k_bf = _proj_headnorm(h, _row(kv_norm), kv_bf, _row(k_norm_w), HD)
                vt = _vproj(h, _row(kv_norm), kv_bf, HD)
                kmean = _kmean(k_bf)
            q = _proj_headnorm(h, _row(moba_norm[b]), moba_w_q[b].astype(BF16),
                               _row(moba_q_norm[b]), HD,
                               scale=MOBA_HEAD_DIM ** -0.5 * LOG2E)
            mask = _gate_mask(q, kmean)
            o = _moba_attn(tbl, q, k_bf, vt, mask)
            h = _mm_res(o, moba_w_out[b].astype(BF16), h)
        h = _conv_ffn(h, _row(ffn_norm[layer]), ffn_w_up[layer].astype(BF16),
                      ffn_conv_w[layer], _row(ffn_conv_b[layer]), ffn_w_down[layer].astype(BF16))
    return h[None]
```

```python
import functools
import math

import jax
import jax.numpy as jnp
from jax import lax
from jax.experimental import pallas as pl
from jax.experimental.pallas import tpu as pltpu

F32 = jnp.float32
BF16 = jnp.bfloat16

EPS = 1e-6
GLA_HEADS = 4
GLA_CHUNK = 64
GLA_GATE_NORMALIZER = 16.0
MOBA_HEADS = 16
MOBA_HEAD_DIM = 128
MOBA_BLOCK = 256
MOBA_TOPK = 3
REL_BUCKETS = 32
REL_MAX_DISTANCE = 128
CONV_WIDTH = 3

VMEM_LIMIT_BYTES = 56 * 1024 * 1024
LANE = 128
HALO_ROWS = 8
BF16_SUBLANES = 16
MOBA_KV_GROUP = 2
MOBA_HEADS_PER_STEP = 2
LOG2E = math.log2(math.e)
FFN_ROW_STRIP = 64
VT_ROWS = MOBA_HEAD_DIM + BF16_SUBLANES


def _cparams(*sem):
    return pltpu.CompilerParams(dimension_semantics=sem,
                                vmem_limit_bytes=VMEM_LIMIT_BYTES)


def _rmsnorm_rows(x, w):
    ms = jnp.mean(x * x, axis=-1, keepdims=True)
    return x * lax.rsqrt(ms + EPS) * w


def _dot(a, b):
    return jnp.dot(a, b, preferred_element_type=F32)


def _dot_nt(a, b):
    return lax.dot_general(a, b, (((1,), (1,)), ((), ())), preferred_element_type=F32)


def _dot_tn(a, b):
    return lax.dot_general(a, b, (((0,), (0,)), ((), ())), preferred_element_type=F32)


def _gla_in_kernel(x_ref, nw_ref, w_ref, w1_ref, w2_ref, b_ref, proj_ref, la_ref, xn_scr):
    @pl.when(pl.program_id(1) == 0)
    def _():
        xn = _rmsnorm_rows(x_ref[...], nw_ref[...]).astype(BF16)
        xn_scr[...] = xn
        r = _dot(xn, w1_ref[...])
        gk = _dot(r.astype(BF16), w2_ref[...]) + b_ref[...]
        log_sig = jnp.minimum(gk, 0.0) - jnp.log1p(jnp.exp(-jnp.abs(gk)))
        la_ref[...] = log_sig * (1.0 / GLA_GATE_NORMALIZER)

    proj_ref[...] = _dot(xn_scr[...], w_ref[...])


def _gla_in(x, norm_w, w_in, w1p, w2p, gk_b, *, tm=1024):
    S, D = x.shape
    N = w_in.shape[1]
    KD = w2p.shape[1]
    tn = KD
    return pl.pallas_call(
        _gla_in_kernel,
        grid=(S // tm, N // tn),
        in_specs=[
            pl.BlockSpec((tm, D), lambda i, j: (i, 0)),
            pl.BlockSpec((1, D), lambda i, j: (0, 0)),
            pl.BlockSpec((D, tn), lambda i, j: (0, j)),
            pl.BlockSpec((D, LANE), lambda i, j: (0, 0)),
            pl.BlockSpec((LANE, KD), lambda i, j: (0, 0)),
            pl.BlockSpec((1, KD), lambda i, j: (0, 0)),
        ],
        out_specs=[
            pl.BlockSpec((None, tm, tn), lambda i, j: (j, i, 0)),
            pl.BlockSpec((tm, KD), lambda i, j: (i, 0)),
        ],
        out_shape=[jax.ShapeDtypeStruct((N // tn, S, tn), F32),
                   jax.ShapeDtypeStruct((S, KD), F32)],
        scratch_shapes=[pltpu.VMEM((tm, D), BF16)],
        compiler_params=_cparams("arbitrary", "arbitrary"),
        name="gla_in",
    )(x, norm_w, w_in, w1p, w2p, gk_b)


def _gla_core_kernel(q_ref, k_ref, v_ref, g_ref, la_ref, onw_ref, o_ref, st_ref, *, n_chunks):
    C = GLA_CHUNK
    H, dv, dk = st_ref.shape

    @pl.when(pl.program_id(0) == 0)
    def _():
        st_ref[...] = jnp.zeros_like(st_ref)

    row = lax.broadcasted_iota(jnp.int32, (C, C), 0)
    col = lax.broadcasted_iota(jnp.int32, (C, C), 1)
    causal = row >= col
    tril = causal.astype(BF16)
    q_scale = dk ** -0.5

    def chunk(c, carry):
        rows = pl.ds(pl.multiple_of(c * C, C), C)
        for h in range(H):
            kc = slice(h * dk, (h + 1) * dk)
            vc = slice(h * dv, (h + 1) * dv)
            per = v_ref.shape[2] // dv
            vchunk, vcc = h // per, slice((h % per) * dv, (h % per + 1) * dv)
            la = la_ref[rows, kc]
            la_hi = la.astype(BF16)
            la_lo = (la - la_hi.astype(F32)).astype(BF16)
            b = _dot(tril, la_hi) + _dot(tril, la_lo)
            b_last = b[C - 1:C, :]
            q = q_ref[rows, kc]
            k = k_ref[rows, kc]
            v = v_ref[vchunk, rows, vcc].astype(BF16)
            q_dec = ((q * q_scale) * jnp.exp(b)).astype(BF16)
            k_inv = (k * jnp.exp(-b)).astype(BF16)
            k_end = (k * jnp.exp(b_last - b)).astype(BF16)
            chunk_decay = jnp.exp(b_last)
            attn = jnp.where(causal, _dot_nt(q_dec, k_inv), 0.0)
            st = st_ref[h]
            o = _dot(attn.astype(BF16), v) + _dot_nt(q_dec, st.astype(BF16))
            st_ref[h] = st * chunk_decay + _dot_tn(v, k_end)
            y = _rmsnorm_rows(o, onw_ref[...])
            g = g_ref[vchunk, rows, vcc]
            y = y * (g * (1.0 / (1.0 + jnp.exp(-g))))
            o_ref[rows, vc] = y.astype(BF16)
        return carry

    lax.fori_loop(0, n_chunks, chunk, 0)


def _gla_core(proj, log_a, o_norm_w, *, tile=256):
    n_col_chunks, S, KD = proj.shape
    H = GLA_HEADS
    assert log_a.shape[1] == KD
    nv = (n_col_chunks - 2) // 2
    assert 2 % nv == 0 and n_col_chunks == 2 + 2 * nv
    VD = nv * KD
    return pl.pallas_call(
        functools.partial(_gla_core_kernel, n_chunks=tile // GLA_CHUNK),
        grid=(S // tile,),
        in_specs=[
            pl.BlockSpec((None, tile, KD), lambda t: (0, t, 0)),
            pl.BlockSpec((None, tile, KD), lambda t: (1, t, 0)),
            pl.BlockSpec((nv, tile, KD), lambda t: (2 // nv, t, 0)),
            pl.BlockSpec((nv, tile, KD), lambda t: ((2 + nv) // nv, t, 0)),
            pl.BlockSpec((tile, KD), lambda t: (t, 0)),
            pl.BlockSpec((1, VD // H), lambda t: (0, 0)),
        ],
        out_specs=pl.BlockSpec((tile, VD), lambda t: (t, 0)),
        out_shape=jax.ShapeDtypeStruct((S, VD), BF16),
        scratch_shapes=[pltpu.VMEM((H, VD // H, KD // H), F32)],
        compiler_params=_cparams("arbitrary"),
        name="gla_core",
    )(proj, proj, proj, proj, log_a, o_norm_w)


def _mm_res_kernel(a_ref, w_ref, r_ref, o_ref):
    o_ref[...] = r_ref[...] + _dot(a_ref[...], w_ref[...])


def _mm_res(a, w, res, *, tm=512, tn=2048):
    S, K = a.shape
    N = w.shape[1]
    return pl.pallas_call(
        _mm_res_kernel,
        grid=(S // tm, N // tn),
        in_specs=[
            pl.BlockSpec((tm, K), lambda i, j: (i, 0)),
            pl.BlockSpec((K, tn), lambda i, j: (0, j)),
            pl.BlockSpec((tm, tn), lambda i, j: (i, j)),
        ],
        out_specs=pl.BlockSpec((tm, tn), lambda i, j: (i, j)),
        out_shape=jax.ShapeDtypeStruct((S, N), F32),
        compiler_params=_cparams("arbitrary", "arbitrary"),
        name="mm_res",
    )(a, w, res)


def _ffn_kernel(x_ref, nw_ref, wa_ref, wu_ref, cwa_ref, cwu_ref, cba_ref, cbu_ref, wd_ref,
                o_ref, xn_scr, halo_a, halo_u, ha0, hu0, ha1, hu1, act0, act1, *, n_chunks, n_steps):
    s = pl.program_id(0)
    last = n_steps - 1
    ja = jnp.minimum(s, last) % n_chunks
    jb = jnp.clip(s - 1, 0, last) % n_chunks
    jc = jnp.clip(s - 2, 0, last) % n_chunks

    @pl.when(s == 0)
    def _():
        for ref in (halo_a, halo_u, ha0, hu0, ha1, hu1, act0, act1):
            ref[...] = jnp.zeros(ref.shape, ref.dtype)

    @pl.when((ja == 0) & (s <= last))
    def _():
        xn_scr[...] = _rmsnorm_rows(x_ref[...], nw_ref[...]).astype(BF16)

    @pl.when(jc == 0)
    def _():
        o_ref[...] = x_ref[...]

    tm, tf = act0.shape
    top = HALO_ROWS

    n = FFN_ROW_STRIP
    sub = HALO_ROWS

    def taps(cw_ref, cb_ref, cols):
        rep = lambda r: jnp.broadcast_to(r, (sub, LANE))
        return (rep(cw_ref[0:1, cols]), rep(cw_ref[1:2, cols]), rep(cw_ref[2:3, cols]),
                rep(cb_ref[:, cols]))

    def conv_rows(h_ref, w, r0, cols):
        win = lambda off: h_ref[top + off + r0:top + off + r0 + n, cols].reshape(n // sub, sub, LANE)
        return w[3] + w[0] * win(-2) + w[1] * win(-1) + w[2] * win(0)

    def stages(h_out, h_in, act_out, act_in):
        for h_ref, halo_ref in ((h_in[0], halo_a), (h_in[1], halo_u)):
            h_ref[0:top, :] = halo_ref[jb]
            halo_ref[jb] = h_ref[tm:tm + top, :]
        for c0 in range(0, tf, LANE):
            cols = slice(c0, c0 + LANE)
            wa_t = taps(cwa_ref, cba_ref, cols)
            wu_t = taps(cwu_ref, cbu_ref, cols)
            for r0 in range(0, tm, n):
                ca = conv_rows(h_in[0], wa_t, r0, cols)
                cu = conv_rows(h_in[1], wu_t, r0, cols)
                act = (ca * (1.0 / (1.0 + jnp.exp(-ca)))) * cu
                act_out[r0:r0 + n, cols] = act.reshape(n, LANE).astype(BF16)
        xn = xn_scr[...]
        h_out[0][top:top + tm, :] = _dot(xn, wa_ref[...])
        h_out[1][top:top + tm, :] = _dot(xn, wu_ref[...])
        o_ref[...] += _dot(act_in[...], wd_ref[...])

    @pl.when(s % 2 == 0)
    def _():
        stages((ha0, hu0), (ha1, hu1), act0, act1)

    @pl.when(s % 2 == 1)
    def _():
        stages((ha1, hu1), (ha0, hu0), act1, act0)


def _conv_ffn(x, norm_w, w_up, conv_w, conv_b, w_down, *, tm=1024, tf=256):
    S, D = x.shape
    Fh = w_down.shape[0]
    nF = Fh // tf
    assert nF > 2
    T = (S // tm) * nF
    last = T - 1

    def a_step(s):
        return jnp.minimum(s, last)

    def b_chunk(s):
        return jnp.clip(s - 1, 0, last) % nF

    def c_step(s):
        return jnp.clip(s - 2, 0, last)

    return pl.pallas_call(
        functools.partial(_ffn_kernel, n_chunks=nF, n_steps=T),
        grid=(T + 2,),
        in_specs=[
            pl.BlockSpec((tm, D), lambda s: (a_step(s) // nF, 0)),
            pl.BlockSpec((1, D), lambda s: (0, 0)),
            pl.BlockSpec((D, tf), lambda s: (0, a_step(s) % nF)),
            pl.BlockSpec((D, tf), lambda s: (0, nF + a_step(s) % nF)),
            pl.BlockSpec((CONV_WIDTH, tf), lambda s: (0, b_chunk(s))),
            pl.BlockSpec((CONV_WIDTH, tf), lambda s: (0, nF + b_chunk(s))),
            pl.BlockSpec((1, tf), lambda s: (0, b_chunk(s))),
            pl.BlockSpec((1, tf), lambda s: (0, nF + b_chunk(s))),
            pl.BlockSpec((tf, D), lambda s: (c_step(s) % nF, 0)),
        ],
        out_specs=pl.BlockSpec((tm, D), lambda s: (c_step(s) // nF, 0)),
        out_shape=jax.ShapeDtypeStruct((S, D), F32),
        scratch_shapes=[pltpu.VMEM((tm, D), BF16),
                        pltpu.VMEM((nF, HALO_ROWS, tf), F32),
                        pltpu.VMEM((nF, HALO_ROWS, tf), F32),
                        pltpu.VMEM((HALO_ROWS + tm, tf), F32), pltpu.VMEM((HALO_ROWS + tm, tf), F32),
                        pltpu.VMEM((HALO_ROWS + tm, tf), F32), pltpu.VMEM((HALO_ROWS + tm, tf), F32),
                        pltpu.VMEM((tm, tf), BF16), pltpu.VMEM((tm, tf), BF16)],
        compiler_params=_cparams("arbitrary"),
        name="conv_ffn",
    )(x, norm_w, w_up, w_up, conv_w, conv_w, conv_b, conv_b, w_down)


def _proj_headnorm_kernel(x_ref, nw_ref, w_ref, hw_ref, o_ref, xn_scr, *, scale):
    @pl.when(pl.program_id(1) == 0)
    def _():
        xn_scr[...] = _rmsnorm_rows(x_ref[...], nw_ref[...]).astype(BF16)

    y = _dot(xn_scr[...], w_ref[...])
    Dh = hw_ref.shape[1]
    for hh in range(y.shape[1] // Dh):
        cols = slice(hh * Dh, (hh + 1) * Dh)
        yn = _rmsnorm_rows(y[:, cols], hw_ref[...])
        if scale is not None:
            yn = yn * scale
        o_ref[:, cols] = yn.astype(BF16)


def _proj_headnorm(x, norm_w, w, head_w, n_out, *, scale=None, tm=512, tn=2048):
    S, D = x.shape
    Dh = head_w.shape[1]
    return pl.pallas_call(
        functools.partial(_proj_headnorm_kernel, scale=scale),
        grid=(S // tm, n_out // tn),
        in_specs=[
            pl.BlockSpec((tm, D), lambda i, j: (i, 0)),
            pl.BlockSpec((1, D), lambda i, j: (0, 0)),
            pl.BlockSpec((D, tn), lambda i, j: (0, j)),
            pl.BlockSpec((1, Dh), lambda i, j: (0, 0)),
        ],
        out_specs=pl.BlockSpec((tm, tn), lambda i, j: (i, j)),
        out_shape=jax.ShapeDtypeStruct((S, n_out), BF16),
        scratch_shapes=[pltpu.VMEM((tm, D), BF16)],
        compiler_params=_cparams("arbitrary", "arbitrary"),
        name="proj_headnorm",
    )(x, norm_w, w, head_w)


def _vproj_kernel(x_ref, nw_ref, w_ref, vt_ref, xn_scr):
    @pl.when(pl.program_id(1) == 0)
    def _():
        xn_scr[...] = _rmsnorm_rows(x_ref[...], nw_ref[...]).astype(BF16)

    y = _dot(xn_scr[...], w_ref[...])
    n_h, n_b, rows, BLK = vt_ref.shape
    Dh = MOBA_HEAD_DIM
    ones = jnp.ones((rows - Dh, BLK), BF16)
    for hh in range(n_h):
        for bb in range(n_b):
            blk = y[bb * BLK:(bb + 1) * BLK, hh * Dh:(hh + 1) * Dh]
            vt_ref[hh, bb, 0:Dh, :] = blk.T.astype(BF16)
            vt_ref[hh, bb, Dh:rows, :] = ones


def _vproj(x, norm_w, w_kv, col0, *, tm=512, tn=2048):
    S, D = x.shape
    H, Dh, BLK = MOBA_HEADS, MOBA_HEAD_DIM, MOBA_BLOCK
    off = col0 // tn
    return pl.pallas_call(
        _vproj_kernel,
        grid=(S // tm, (H * Dh) // tn),
        in_specs=[
            pl.BlockSpec((tm, D), lambda i, j: (i, 0)),
            pl.BlockSpec((1, D), lambda i, j: (0, 0)),
            pl.BlockSpec((D, tn), lambda i, j: (0, off + j)),
        ],
        out_specs=pl.BlockSpec((tn // Dh, tm // BLK, VT_ROWS, BLK), lambda i, j: (j, i, 0, 0)),
        out_shape=jax.ShapeDtypeStruct((H, S // BLK, VT_ROWS, BLK), BF16),
        scratch_shapes=[pltpu.VMEM((tm, D), BF16)],
        compiler_params=_cparams("arbitrary", "arbitrary"),
        name="v_proj",
    )(x, norm_w, w_kv)


def _kmean_kernel(k_ref, o_ref):
    rows, cols = k_ref.shape
    k = k_ref[...].astype(F32).reshape(rows // MOBA_BLOCK, MOBA_BLOCK, cols)
    o_ref[...] = jnp.sum(k, axis=1) * (1.0 / MOBA_BLOCK)


def _kmean(k, *, tr=2048, tc=512):
    S, N = k.shape
    return pl.pallas_call(
        _kmean_kernel,
        grid=(S // tr, N // tc),
        in_specs=[pl.BlockSpec((tr, tc), lambda i, j: (i, j))],
        out_specs=pl.BlockSpec((tr // MOBA_BLOCK, tc), lambda i, j: (i, j)),
        out_shape=jax.ShapeDtypeStruct((S // MOBA_BLOCK, N), F32),
        compiler_params=_cparams("arbitrary", "arbitrary"),
        name="kmean",
    )(k)


def _gate_kernel(q_ref, km_ref, mask_ref):
    nb = km_ref.shape[0]
    tq = q_ref.shape[0]
    g = _dot_nt(km_ref[...].astype(BF16), q_ref[...])
    blk = lax.broadcasted_iota(jnp.int32, (nb, tq), 0)
    pos = pl.program_id(1) * tq + lax.broadcasted_iota(jnp.int32, (nb, tq), 1)
    own = lax.shift_right_logical(pos, int(math.log2(MOBA_BLOCK)))
    neg = -jnp.inf
    v = jnp.where(blk < own, g, neg)
    sel = jnp.zeros((nb, tq), jnp.bool_)
    for r in range(min(MOBA_TOPK, nb)):
        m = jnp.max(v, axis=0, keepdims=True)
        first = jnp.min(jnp.where(v == m, blk, nb), axis=0, keepdims=True)
        onehot = blk == first
        sel = sel | (onehot & (own > r))
        v = jnp.where(onehot, neg, v)
    mask_ref[0] = jnp.where(sel, 0.0, neg)


def _gate_mask(q, kmean, *, tq=2048):
    S = q.shape[0]
    H, Dh = MOBA_HEADS, MOBA_HEAD_DIM
    nb = kmean.shape[0]
    return pl.pallas_call(
        _gate_kernel,
        grid=(H, S // tq),
        in_specs=[
            pl.BlockSpec((tq, Dh), lambda h, t: (t, h)),
            pl.BlockSpec((nb, Dh), lambda h, t: (0, h)),
        ],
        out_specs=pl.BlockSpec((1, nb, tq), lambda h, t: (h, 0, t)),
        out_shape=jax.ShapeDtypeStruct((H, nb, S), F32),
        compiler_params=_cparams("arbitrary", "arbitrary"),
        name="gate_topk",
    )(q, kmean)


def _rel_bias_tile(tbl_ref, h, dist):
    max_exact = REL_BUCKETS // 2
    n = jnp.maximum(dist, 0)
    large = max_exact + (jnp.log(jnp.maximum(n, 1).astype(F32) / max_exact)
                         / math.log(REL_MAX_DISTANCE / max_exact)
                         * (REL_BUCKETS - max_exact)).astype(jnp.int32)
    large = jnp.minimum(large, REL_BUCKETS - 1)
    bucket = jnp.where(n < max_exact, n, large)
    out = jnp.zeros(dist.shape, F32)
    for b in range(REL_BUCKETS):
        out = jnp.where(bucket == b, tbl_ref[h, b], out)
    return out


def _moba_kernel(tbl_ref, q_ref, k_ref, vt_ref, mask_ref, o_ref,
                 bown_ref, bprev_ref, sa_ref, sb_ref):
    BLK, G, Dh = MOBA_BLOCK, MOBA_KV_GROUP, MOBA_HEAD_DIM
    NH, nb = vt_ref.shape[0], vt_ref.shape[1]
    STEP = 2 * G
    head0 = pl.program_id(0) * NH
    i = pl.program_id(1)
    neg = -jnp.inf

    @pl.when(i == 0)
    def _():
        key = lax.broadcasted_iota(jnp.int32, (BLK, BLK), 0)
        qry = lax.broadcasted_iota(jnp.int32, (BLK, BLK), 1)
        dist = qry - key
        for hh in range(NH):
            own = _rel_bias_tile(tbl_ref, head0 + hh, dist) * LOG2E
            bown_ref[hh] = jnp.where(dist >= 0, own, neg)
            bprev_ref[hh] = _rel_bias_tile(tbl_ref, head0 + hh, dist + BLK) * LOG2E

    q = [q_ref[:, hh * Dh:(hh + 1) * Dh] for hh in range(NH)]

    def k_rows(hh, j, n):
        return k_ref[pl.ds(pl.multiple_of(j * BLK, BLK), n * BLK), hh * Dh:(hh + 1) * Dh]

    def colmax(s):
        return jnp.max(s, axis=0, keepdims=True)

    def put_scores(s_ref, j0):
        for hh in range(NH):
            s_ref[hh] = _dot_nt(k_rows(hh, j0, G), q[hh])

    put_scores(sa_ref, 0)

    jp = jnp.maximum(i - 1, 0)
    carry = []
    for hh in range(NH):
        s_own = _dot_nt(k_rows(hh, i, 1), q[hh]) + bown_ref[hh]
        s_prev = _dot_nt(k_rows(hh, jp, 1), q[hh]) + bprev_ref[hh]
        sel_prev = mask_ref[hh, pl.ds(jp, 1), :]
        m = jnp.maximum(colmax(s_own), colmax(s_prev) + sel_prev)
        p_own = jnp.exp2(s_own - m).astype(BF16)
        p_prev = jnp.exp2(s_prev - (m - sel_prev)).astype(BF16)
        carry += [m, _dot(vt_ref[hh, i], p_own) + _dot(vt_ref[hh, jp], p_prev)]

    far_bias = [tbl_ref[head0 + hh, REL_BUCKETS - 1] * LOG2E for hh in range(NH)]

    def update(s_ref, j0, carry):
        out = []
        for hh in range(NH):
            m, acc = carry[2 * hh], carry[2 * hh + 1]
            sel = []
            m_new = m
            for g in range(G):
                row = (mask_ref[hh, pl.ds(j0 + g, 1), :]
                       + jnp.where(j0 + g < i - 1, far_bias[hh], neg))
                sel.append(row)
                m_new = jnp.maximum(m_new, colmax(s_ref[hh, g * BLK:(g + 1) * BLK, :]) + row)
            acc = acc * jnp.exp2(m - m_new)
            for g in range(G):
                p = jnp.exp2(s_ref[hh, g * BLK:(g + 1) * BLK, :] - (m_new - sel[g])).astype(BF16)
                acc = acc + _dot(vt_ref[hh, j0 + g], p)
            out += [m_new, acc]
        return out

    def trip(t, carry):
        j0 = t * STEP
        put_scores(sb_ref, j0 + G)
        carry = update(sa_ref, j0, list(carry))
        put_scores(sa_ref, jnp.minimum(j0 + STEP, nb - G))
        carry = update(sb_ref, j0 + G, carry)
        return tuple(carry)

    n_trips = (i + (STEP - 2)) // STEP
    carry = lax.fori_loop(0, n_trips, trip, tuple(carry))
    for hh in range(NH):
        acc = carry[2 * hh + 1]
        o_ref[:, hh * Dh:(hh + 1) * Dh] = (acc[:Dh] / acc[Dh:Dh + 1]).T.astype(BF16)


def _moba_attn(tbl, q, k, vt, mask):
    S = q.shape[0]
    H, Dh, BLK = MOBA_HEADS, MOBA_HEAD_DIM, MOBA_BLOCK
    nb = S // BLK
    NH, G = MOBA_HEADS_PER_STEP, MOBA_KV_GROUP
    return pl.pallas_call(
        _moba_kernel,
        grid=(H // NH, nb),
        in_specs=[
            pl.BlockSpec(memory_space=pltpu.SMEM),
            pl.BlockSpec((BLK, NH * Dh), lambda h, i: (i, h)),
            pl.BlockSpec((S, NH * Dh), lambda h, i: (0, h)),
            pl.BlockSpec((NH, nb, VT_ROWS, BLK), lambda h, i: (h, 0, 0, 0)),
            pl.BlockSpec((NH, nb, BLK), lambda h, i: (h, 0, i)),
        ],
        out_specs=pl.BlockSpec((BLK, NH * Dh), lambda h, i: (i, h)),
        out_shape=jax.ShapeDtypeStruct((S, H * Dh), BF16),
        scratch_shapes=[pltpu.VMEM((NH, BLK, BLK), F32), pltpu.VMEM((NH, BLK, BLK), F32),
                        pltpu.VMEM((NH, G * BLK, BLK), F32), pltpu.VMEM((NH, G * BLK, BLK), F32)],
        compiler_params=_cparams("arbitrary", "arbitrary"),
        name="moba_attn",
    )(tbl, q, k, vt, mask)


def _row(v):
    return v.reshape(1, -1)


def kernel(x, gla_norm, gla_w_in, gla_gk_w1, gla_gk_w2, gla_gk_b, gla_o_norm, gla_w_out,
           kv_norm, kv_w, k_norm_w, moba_norm, moba_w_q, moba_q_norm, moba_w_out, rel_bias,
           ffn_norm, ffn_w_up, ffn_conv_w, ffn_conv_b, ffn_w_down):
    B, S, D = x.shape
    assert B == 1
    depth = ffn_norm.shape[0]
    n_a = gla_norm.shape[0]
    h = x[0]
    HD = MOBA_HEADS * MOBA_HEAD_DIM
    k_bf = vt = kmean = None
    tbl = rel_bias.T

    for layer in range(depth):
        if layer < n_a:
            a = layer
            R = gla_gk_w1.shape[2]
            w1p = jnp.pad(gla_gk_w1[a].astype(BF16), ((0, 0), (0, LANE - R)))
            w2p = jnp.pad(gla_gk_w2[a].astype(BF16), ((0, LANE - R), (0, 0)))
            proj, log_a = _gla_in(h, _row(gla_norm[a]), gla_w_in[a].astype(BF16), w1p, w2p,
                                  _row(gla_gk_b[a]))
            o = _gla_core(proj, log_a, _row(gla_o_norm[a]))
            h = _mm_res(o, gla_w_out[a].astype(BF16), h)
        else:
            b = layer - n_a
            if k_bf is None:
                kv_bf = kv_w.astype(BF16)
                k_bf = _proj_headnorm(h, _row(kv_norm), kv_bf, _row(k_norm_w), HD)
                vt = _vproj(h, _row(kv_norm), kv_bf, HD)
                kmean = _kmean(k_bf)
            q = _proj_headnorm(h, _row(moba_norm[b]), moba_w_q[b].astype(BF16),
                               _row(moba_q_norm[b]), HD,
                               scale=MOBA_HEAD_DIM ** -0.5 * LOG2E)
            mask = _gate_mask(q, kmean)
            o = _moba_attn(tbl, q, k_bf, vt, mask)
            h = _mm_res(o, moba_w_out[b].astype(BF16), h)
        h = _conv_ffn(h, _row(ffn_norm[layer]), ffn_w_up[layer].astype(BF16),
                      ffn_conv_w[layer], _row(ffn_conv_b[layer]), ffn_w_down[layer].astype(BF16))
    return h[None]
```

```python
import functools
import math

import jax
import jax.numpy as jnp
from jax import lax
from jax.experimental import pallas as pl
from jax.experimental.pallas import tpu as pltpu

F32 = jnp.float32
BF16 = jnp.bfloat16

EPS = 1e-6
GLA_HEADS = 4
GLA_CHUNK = 64
GLA_GATE_NORMALIZER = 16.0
MOBA_HEADS = 16
MOBA_HEAD_DIM = 128
MOBA_BLOCK = 256
MOBA_TOPK = 3
REL_BUCKETS = 32
REL_MAX_DISTANCE = 128
CONV_WIDTH = 3

VMEM_LIMIT_BYTES = 56 * 1024 * 1024
LANE = 128
HALO_ROWS = 8
BF16_SUBLANES = 16
MOBA_KV_GROUP = 1
MOBA_SCORE_BUFFERS = 3
MOBA_ROTATIONS = 2
MOBA_HEADS_PER_STEP = 2
LOG2E = math.log2(math.e)
FFN_ROW_STRIP = 64
VT_ROWS = MOBA_HEAD_DIM + BF16_SUBLANES


def _cparams(*sem):
    return pltpu.CompilerParams(dimension_semantics=sem,
                                vmem_limit_bytes=VMEM_LIMIT_BYTES)


def _rmsnorm_rows(x, w):
    ms = jnp.mean(x * x, axis=-1, keepdims=True)
    return x * lax.rsqrt(ms + EPS) * w


def _dot(a, b):
    return jnp.dot(a, b, preferred_element_type=F32)


def _dot_nt(a, b):
    return lax.dot_general(a, b, (((1,), (1,)), ((), ())), preferred_element_type=F32)


def _dot_tn(a, b):
    return lax.dot_general(a, b, (((0,), (0,)), ((), ())), preferred_element_type=F32)


def _gla_in_kernel(x_ref, nw_ref, w_ref, w1_ref, w2_ref, b_ref, proj_ref, la_ref, xn_scr):
    @pl.when(pl.program_id(1) == 0)
    def _():
        xn = _rmsnorm_rows(x_ref[...], nw_ref[...]).astype(BF16)
        xn_scr[...] = xn
        r = _dot(xn, w1_ref[...])
        gk = _dot(r.astype(BF16), w2_ref[...]) + b_ref[...]
        log_sig = jnp.minimum(gk, 0.0) - jnp.log1p(jnp.exp(-jnp.abs(gk)))
        la_ref[...] = log_sig * (1.0 / GLA_GATE_NORMALIZER)

    proj_ref[...] = _dot(xn_scr[...], w_ref[...])


def _gla_in(x, norm_w, w_in, w1p, w2p, gk_b, *, tm=1024):
    S, D = x.shape
    N = w_in.shape[1]
    KD = w2p.shape[1]
    tn = KD
    return pl.pallas_call(
        _gla_in_kernel,
        grid=(S // tm, N // tn),
        in_specs=[
            pl.BlockSpec((tm, D), lambda i, j: (i, 0)),
            pl.BlockSpec((1, D), lambda i, j: (0, 0)),
            pl.BlockSpec((D, tn), lambda i, j: (0, j)),
            pl.BlockSpec((D, LANE), lambda i, j: (0, 0)),
            pl.BlockSpec((LANE, KD), lambda i, j: (0, 0)),
            pl.BlockSpec((1, KD), lambda i, j: (0, 0)),
        ],
        out_specs=[
            pl.BlockSpec((None, tm, tn), lambda i, j: (j, i, 0)),
            pl.BlockSpec((tm, KD), lambda i, j: (i, 0)),
        ],
        out_shape=[jax.ShapeDtypeStruct((N // tn, S, tn), F32),
                   jax.ShapeDtypeStruct((S, KD), F32)],
        scratch_shapes=[pltpu.VMEM((tm, D), BF16)],
        compiler_params=_cparams("arbitrary", "arbitrary"),
        name="gla_in",
    )(x, norm_w, w_in, w1p, w2p, gk_b)


def _gla_core_kernel(q_ref, k_ref, v_ref, g_ref, la_ref, onw_ref, o_ref, st_ref, *, n_chunks):
    C = GLA_CHUNK
    H, dv, dk = st_ref.shape

    @pl.when(pl.program_id(0) == 0)
    def _():
        st_ref[...] = jnp.zeros_like(st_ref)

    row = lax.broadcasted_iota(jnp.int32, (C, C), 0)
    col = lax.broadcasted_iota(jnp.int32, (C, C), 1)
    causal = row >= col
    tril = causal.astype(BF16)
    q_scale = dk ** -0.5

    def chunk(c, carry):
        rows = pl.ds(pl.multiple_of(c * C, C), C)
        for h in range(H):
            kc = slice(h * dk, (h + 1) * dk)
            vc = slice(h * dv, (h + 1) * dv)
            per = v_ref.shape[2] // dv
            vchunk, vcc = h // per, slice((h % per) * dv, (h % per + 1) * dv)
            la = la_ref[rows, kc]
            la_hi = la.astype(BF16)
            la_lo = (la - la_hi.astype(F32)).astype(BF16)
            b = _dot(tril, la_hi) + _dot(tril, la_lo)
            b_last = b[C - 1:C, :]
            q = q_ref[rows, kc]
            k = k_ref[rows, kc]
            v = v_ref[vchunk, rows, vcc].astype(BF16)
            q_dec = ((q * q_scale) * jnp.exp(b)).astype(BF16)
            k_inv = (k * jnp.exp(-b)).astype(BF16)
            k_end = (k * jnp.exp(b_last - b)).astype(BF16)
            chunk_decay = jnp.exp(b_last)
            attn = jnp.where(causal, _dot_nt(q_dec, k_inv), 0.0)
            st = st_ref[h]
            o = _dot(attn.astype(BF16), v) + _dot_nt(q_dec, st.astype(BF16))
            st_ref[h] = st * chunk_decay + _dot_tn(v, k_end)
            y = _rmsnorm_rows(o, onw_ref[...])
            g = g_ref[vchunk, rows, vcc]
            y = y * (g * (1.0 / (1.0 + jnp.exp(-g))))
            o_ref[rows, vc] = y.astype(BF16)
        return carry

    lax.fori_loop(0, n_chunks, chunk, 0)


def _gla_core(proj, log_a, o_norm_w, *, tile=256):
    n_col_chunks, S, KD = proj.shape
    H = GLA_HEADS
    assert log_a.shape[1] == KD
    nv = (n_col_chunks - 2) // 2
    assert 2 % nv == 0 and n_col_chunks == 2 + 2 * nv
    VD = nv * KD
    return pl.pallas_call(
        functools.partial(_gla_core_kernel, n_chunks=tile // GLA_CHUNK),
        grid=(S // tile,),
        in_specs=[
            pl.BlockSpec((None, tile, KD), lambda t: (0, t, 0)),
            pl.BlockSpec((None, tile, KD), lambda t: (1, t, 0)),
            pl.BlockSpec((nv, tile, KD), lambda t: (2 // nv, t, 0)),
            pl.BlockSpec((nv, tile, KD), lambda t: ((2 + nv) // nv, t, 0)),
            pl.BlockSpec((tile, KD), lambda t: (t, 0)),
            pl.BlockSpec((1, VD // H), lambda t: (0, 0)),
        ],
        out_specs=pl.BlockSpec((tile, VD), lambda t: (t, 0)),
        out_shape=jax.ShapeDtypeStruct((S, VD), BF16),
        scratch_shapes=[pltpu.VMEM((H, VD // H, KD // H), F32)],
        compiler_params=_cparams("arbitrary"),
        name="gla_core",
    )(proj, proj, proj, proj, log_a, o_norm_w)


def _mm_res_kernel(a_ref, w_ref, r_ref, o_ref):
    o_ref[...] = r_ref[...] + _dot(a_ref[...], w_ref[...])


def _mm_res(a, w, res, *, tm=512, tn=2048):
    S, K = a.shape
    N = w.shape[1]
    return pl.pallas_call(
        _mm_res_kernel,
        grid=(S // tm, N // tn),
        in_specs=[
            pl.BlockSpec((tm, K), lambda i, j: (i, 0)),
            pl.BlockSpec((K, tn), lambda i, j: (0, j)),
            pl.BlockSpec((tm, tn), lambda i, j: (i, j)),
        ],
        out_specs=pl.BlockSpec((tm, tn), lambda i, j: (i, j)),
        out_shape=jax.ShapeDtypeStruct((S, N), F32),
        compiler_params=_cparams("arbitrary", "arbitrary"),
        name="mm_res",
    )(a, w, res)


def _ffn_kernel(x_ref, nw_ref, wa_ref, wu_ref, cwa_ref, cwu_ref, cba_ref, cbu_ref, wd_ref,
                o_ref, xn_scr, halo_a, halo_u, ha0, hu0, ha1, hu1, act0, act1, *, n_chunks, n_steps):
    s = pl.program_id(0)
    last = n_steps - 1
    ja = jnp.minimum(s, last) % n_chunks
    jb = jnp.clip(s - 1, 0, last) % n_chunks
    jc = jnp.clip(s - 2, 0, last) % n_chunks

    @pl.when(s == 0)
    def _():
        for ref in (halo_a, halo_u, ha0, hu0, ha1, hu1, act0, act1):
            ref[...] = jnp.zeros(ref.shape, ref.dtype)

    @pl.when((ja == 0) & (s <= last))
    def _():
        xn_scr[...] = _rmsnorm_rows(x_ref[...], nw_ref[...]).astype(BF16)

    @pl.when(jc == 0)
    def _():
        o_ref[...] = x_ref[...]

    tm, tf = act0.shape
    top = HALO_ROWS

    n = FFN_ROW_STRIP
    sub = HALO_ROWS

    def taps(cw_ref, cb_ref, cols):
        rep = lambda r: jnp.broadcast_to(r, (sub, LANE))
        return (rep(cw_ref[0:1, cols]), rep(cw_ref[1:2, cols]), rep(cw_ref[2:3, cols]),
                rep(cb_ref[:, cols]))

    def conv_rows(h_ref, w, r0, cols):
        win = lambda off: h_ref[top + off + r0:top + off + r0 + n, cols].reshape(n // sub, sub, LANE)
        return w[3] + w[0] * win(-2) + w[1] * win(-1) + w[2] * win(0)

    def stages(h_out, h_in, act_out, act_in):
        for h_ref, halo_ref in ((h_in[0], halo_a), (h_in[1], halo_u)):
            h_ref[0:top, :] = halo_ref[jb]
            halo_ref[jb] = h_ref[tm:tm + top, :]
        for c0 in range(0, tf, LANE):
            cols = slice(c0, c0 + LANE)
            wa_t = taps(cwa_ref, cba_ref, cols)
            wu_t = taps(cwu_ref, cbu_ref, cols)
            for r0 in range(0, tm, n):
                ca = conv_rows(h_in[0], wa_t, r0, cols)
                cu = conv_rows(h_in[1], wu_t, r0, cols)
                act = (ca * (1.0 / (1.0 + jnp.exp(-ca)))) * cu
                act_out[r0:r0 + n, cols] = act.reshape(n, LANE).astype(BF16)
        xn = xn_scr[...]
        h_out[0][top:top + tm, :] = _dot(xn, wa_ref[...])
        h_out[1][top:top + tm, :] = _dot(xn, wu_ref[...])
        o_ref[...] += _dot(act_in[...], wd_ref[...])

    @pl.when(s % 2 == 0)
    def _():
        stages((ha0, hu0), (ha1, hu1), act0, act1)

    @pl.when(s % 2 == 1)
    def _():
        stages((ha1, hu1), (ha0, hu0), act1, act0)


def _conv_ffn(x, norm_w, w_up, conv_w, conv_b, w_down, *, tm=1024, tf=256):
    S, D = x.shape
    Fh = w_down.shape[0]
    nF = Fh // tf
    assert nF > 2
    T = (S // tm) * nF
    last = T - 1

    def a_step(s):
        return jnp.minimum(s, last)

    def b_chunk(s):
        return jnp.clip(s - 1, 0, last) % nF

    def c_step(s):
        return jnp.clip(s - 2, 0, last)

    return pl.pallas_call(
        functools.partial(_ffn_kernel, n_chunks=nF, n_steps=T),
        grid=(T + 2,),
        in_specs=[
            pl.BlockSpec((tm, D), lambda s: (a_step(s) // nF, 0)),
            pl.BlockSpec((1, D), lambda s: (0, 0)),
            pl.BlockSpec((D, tf), lambda s: (0, a_step(s) % nF)),
            pl.BlockSpec((D, tf), lambda s: (0, nF + a_step(s) % nF)),
            pl.BlockSpec((CONV_WIDTH, tf), lambda s: (0, b_chunk(s))),
            pl.BlockSpec((CONV_WIDTH, tf), lambda s: (0, nF + b_chunk(s))),
            pl.BlockSpec((1, tf), lambda s: (0, b_chunk(s))),
            pl.BlockSpec((1, tf), lambda s: (0, nF + b_chunk(s))),
            pl.BlockSpec((tf, D), lambda s: (c_step(s) % nF, 0)),
        ],
        out_specs=pl.BlockSpec((tm, D), lambda s: (c_step(s) // nF, 0)),
        out_shape=jax.ShapeDtypeStruct((S, D), F32),
        scratch_shapes=[pltpu.VMEM((tm, D), BF16),
                        pltpu.VMEM((nF, HALO_ROWS, tf), F32),
                        pltpu.VMEM((nF, HALO_ROWS, tf), F32),
                        pltpu.VMEM((HALO_ROWS + tm, tf), F32), pltpu.VMEM((HALO_ROWS + tm, tf), F32),
                        pltpu.VMEM((HALO_ROWS + tm, tf), F32), pltpu.VMEM((HALO_ROWS + tm, tf), F32),
                        pltpu.VMEM((tm, tf), BF16), pltpu.VMEM((tm, tf), BF16)],
        compiler_params=_cparams("arbitrary"),
        name="conv_ffn",
    )(x, norm_w, w_up, w_up, conv_w, conv_w, conv_b, conv_b, w_down)


def _proj_headnorm_kernel(x_ref, nw_ref, w_ref, hw_ref, o_ref, xn_scr, *, scale):
    @pl.when(pl.program_id(1) == 0)
    def _():
        xn_scr[...] = _rmsnorm_rows(x_ref[...], nw_ref[...]).astype(BF16)

    y = _dot(xn_scr[...], w_ref[...])
    Dh = hw_ref.shape[1]
    for hh in range(y.shape[1] // Dh):
        cols = slice(hh * Dh, (hh + 1) * Dh)
        yn = _rmsnorm_rows(y[:, cols], hw_ref[...])
        if scale is not None:
            yn = yn * scale
        o_ref[:, cols] = yn.astype(BF16)


def _proj_headnorm(x, norm_w, w, head_w, n_out, *, scale=None, tm=512, tn=2048):
    S, D = x.shape
    Dh = head_w.shape[1]
    return pl.pallas_call(
        functools.partial(_proj_headnorm_kernel, scale=scale),
        grid=(S // tm, n_out // tn),
        in_specs=[
            pl.BlockSpec((tm, D), lambda i, j: (i, 0)),
            pl.BlockSpec((1, D), lambda i, j: (0, 0)),
            pl.BlockSpec((D, tn), lambda i, j: (0, j)),
            pl.BlockSpec((1, Dh), lambda i, j: (0, 0)),
        ],
        out_specs=pl.BlockSpec((tm, tn), lambda i, j: (i, j)),
        out_shape=jax.ShapeDtypeStruct((S, n_out), BF16),
        scratch_shapes=[pltpu.VMEM((tm, D), BF16)],
        compiler_params=_cparams("arbitrary", "arbitrary"),
        name="proj_headnorm",
    )(x, norm_w, w, head_w)


def _vproj_kernel(x_ref, nw_ref, w_ref, vt_ref, xn_scr):
    @pl.when(pl.program_id(1) == 0)
    def _():
        xn_scr[...] = _rmsnorm_rows(x_ref[...], nw_ref[...]).astype(BF16)

    y = _dot(xn_scr[...], w_ref[...])
    n_h, n_b, rows, BLK = vt_ref.shape
    Dh = MOBA_HEAD_DIM
    ones = jnp.ones((rows - Dh, BLK), BF16)
    for hh in range(n_h):
        for bb in range(n_b):
            blk = y[bb * BLK:(bb + 1) * BLK, hh * Dh:(hh + 1) * Dh]
            vt_ref[hh, bb, 0:Dh, :] = blk.T.astype(BF16)
            vt_ref[hh, bb, Dh:rows, :] = ones


def _vproj(x, norm_w, w_kv, col0, *, tm=512, tn=2048):
    S, D = x.shape
    H, Dh, BLK = MOBA_HEADS, MOBA_HEAD_DIM, MOBA_BLOCK
    off = col0 // tn
    return pl.pallas_call(
        _vproj_kernel,
        grid=(S // tm, (H * Dh) // tn),
        in_specs=[
            pl.BlockSpec((tm, D), lambda i, j: (i, 0)),
            pl.BlockSpec((1, D), lambda i, j: (0, 0)),
            pl.BlockSpec((D, tn), lambda i, j: (0, off + j)),
        ],
        out_specs=pl.BlockSpec((tn // Dh, tm // BLK, VT_ROWS, BLK), lambda i, j: (j, i, 0, 0)),
        out_shape=jax.ShapeDtypeStruct((H, S // BLK, VT_ROWS, BLK), BF16),
        scratch_shapes=[pltpu.VMEM((tm, D), BF16)],
        compiler_params=_cparams("arbitrary", "arbitrary"),
        name="v_proj",
    )(x, norm_w, w_kv)


def _kmean_kernel(k_ref, o_ref):
    rows, cols = k_ref.shape
    k = k_ref[...].astype(F32).reshape(rows // MOBA_BLOCK, MOBA_BLOCK, cols)
    o_ref[...] = jnp.sum(k, axis=1) * (1.0 / MOBA_BLOCK)


def _kmean(k, *, tr=2048, tc=512):
    S, N = k.shape
    return pl.pallas_call(
        _kmean_kernel,
        grid=(S // tr, N // tc),
        in_specs=[pl.BlockSpec((tr, tc), lambda i, j: (i, j))],
        out_specs=pl.BlockSpec((tr // MOBA_BLOCK, tc), lambda i, j: (i, j)),
        out_shape=jax.ShapeDtypeStruct((S // MOBA_BLOCK, N), F32),
        compiler_params=_cparams("arbitrary", "arbitrary"),
        name="kmean",
    )(k)


def _gate_kernel(q_ref, km_ref, mask_ref):
    nb = km_ref.shape[0]
    tq = q_ref.shape[0]
    g = _dot_nt(km_ref[...].astype(BF16), q_ref[...])
    blk = lax.broadcasted_iota(jnp.int32, (nb, tq), 0)
    pos = pl.program_id(1) * tq + lax.broadcasted_iota(jnp.int32, (nb, tq), 1)
    own = lax.shift_right_logical(pos, int(math.log2(MOBA_BLOCK)))
    neg = -jnp.inf
    v = jnp.where(blk < own, g, neg)
    sel = jnp.zeros((nb, tq), jnp.bool_)
    for r in range(min(MOBA_TOPK, nb)):
        m = jnp.max(v, axis=0, keepdims=True)
        first = jnp.min(jnp.where(v == m, blk, nb), axis=0, keepdims=True)
        onehot = blk == first
        sel = sel | (onehot & (own > r))
        v = jnp.where(onehot, neg, v)
    mask_ref[0] = jnp.where(sel, 0.0, neg)


def _gate_mask(q, kmean, *, tq=2048):
    S = q.shape[0]
    H, Dh = MOBA_HEADS, MOBA_HEAD_DIM
    nb = kmean.shape[0]
    return pl.pallas_call(
        _gate_kernel,
        grid=(H, S // tq),
        in_specs=[
            pl.BlockSpec((tq, Dh), lambda h, t: (t, h)),
            pl.BlockSpec((nb, Dh), lambda h, t: (0, h)),
        ],
        out_specs=pl.BlockSpec((1, nb, tq), lambda h, t: (h, 0, t)),
        out_shape=jax.ShapeDtypeStruct((H, nb, S), F32),
        compiler_params=_cparams("arbitrary", "arbitrary"),
        name="gate_topk",
    )(q, kmean)


def _rel_bias_tile(tbl_ref, h, dist):
    max_exact = REL_BUCKETS // 2
    n = jnp.maximum(dist, 0)
    large = max_exact + (jnp.log(jnp.maximum(n, 1).astype(F32) / max_exact)
                         / math.log(REL_MAX_DISTANCE / max_exact)
                         * (REL_BUCKETS - max_exact)).astype(jnp.int32)
    large = jnp.minimum(large, REL_BUCKETS - 1)
    bucket = jnp.where(n < max_exact, n, large)
    out = jnp.zeros(dist.shape, F32)
    for b in range(REL_BUCKETS):
        out = jnp.where(bucket == b, tbl_ref[h, b], out)
    return out


def _moba_kernel(tbl_ref, q_ref, k_ref, vt_ref, mask_ref, o_ref,
                 bown_ref, bprev_ref, *score_bufs):
    BLK, G, Dh = MOBA_BLOCK, MOBA_KV_GROUP, MOBA_HEAD_DIM
    NH, nb = vt_ref.shape[0], vt_ref.shape[1]
    NBUF = len(score_bufs)
    PHASES = NBUF * MOBA_ROTATIONS
    STEP = PHASES * G
    head0 = pl.program_id(0) * NH
    i = pl.program_id(1)
    neg = -jnp.inf

    @pl.when(i == 0)
    def _():
        key = lax.broadcasted_iota(jnp.int32, (BLK, BLK), 0)
        qry = lax.broadcasted_iota(jnp.int32, (BLK, BLK), 1)
        dist = qry - key
        for hh in range(NH):
            own = _rel_bias_tile(tbl_ref, head0 + hh, dist) * LOG2E
            bown_ref[hh] = jnp.where(dist >= 0, own, neg)
            bprev_ref[hh] = _rel_bias_tile(tbl_ref, head0 + hh, dist + BLK) * LOG2E

    q = [q_ref[:, hh * Dh:(hh + 1) * Dh] for hh in range(NH)]

    def k_rows(hh, j, n):
        return k_ref[pl.ds(pl.multiple_of(j * BLK, BLK), n * BLK), hh * Dh:(hh + 1) * Dh]

    def colmax(s):
        return jnp.max(s, axis=0, keepdims=True)

    def put_scores(s_ref, j0):
        j0 = jnp.minimum(j0, nb - G)
        for hh in range(NH):
            s_ref[hh] = _dot_nt(k_rows(hh, j0, G), q[hh])

    for b in range(NBUF - 1):
        put_scores(score_bufs[b], b * G)

    jp = jnp.maximum(i - 1, 0)
    carry = []
    for hh in range(NH):
        s_own = _dot_nt(k_rows(hh, i, 1), q[hh]) + bown_ref[hh]
        s_prev = _dot_nt(k_rows(hh, jp, 1), q[hh]) + bprev_ref[hh]
        sel_prev = mask_ref[hh, pl.ds(jp, 1), :]
        m = jnp.maximum(colmax(s_own), colmax(s_prev) + sel_prev)
        p_own = jnp.exp2(s_own - m).astype(BF16)
        p_prev = jnp.exp2(s_prev - (m - sel_prev)).astype(BF16)
        carry += [m, _dot(vt_ref[hh, i], p_own) + _dot(vt_ref[hh, jp], p_prev)]

    far_bias = [tbl_ref[head0 + hh, REL_BUCKETS - 1] * LOG2E for hh in range(NH)]

    def update(s_ref, j0, carry):
        out = []
        for hh in range(NH):
            m, acc = carry[2 * hh], carry[2 * hh + 1]
            sel = []
            m_new = m
            for g in range(G):
                jl = jnp.minimum(j0 + g, nb - 1)
                row = (mask_ref[hh, pl.ds(jl, 1), :]
                       + jnp.where(j0 + g < i - 1, far_bias[hh], neg))
                sel.append(row)
                m_new = jnp.maximum(m_new, colmax(s_ref[hh, g * BLK:(g + 1) * BLK, :]) + row)
            acc = acc * jnp.exp2(m - m_new)
            for g in range(G):
                p = jnp.exp2(s_ref[hh, g * BLK:(g + 1) * BLK, :] - (m_new - sel[g])).astype(BF16)
                acc = acc + _dot(vt_ref[hh, jnp.minimum(j0 + g, nb - 1)], p)
            out += [m_new, acc]
        return out

    def trip(t, carry):
        j0 = t * STEP
        carry = list(carry)
        for ph in range(PHASES):
            put_scores(score_bufs[(ph + NBUF - 1) % NBUF], j0 + (ph + NBUF - 1) * G)
            carry = update(score_bufs[ph % NBUF], j0 + ph * G, carry)
        return tuple(carry)

    n_trips = (i + (STEP - 2)) // STEP
    carry = lax.fori_loop(0, n_trips, trip, tuple(carry))
    for hh in range(NH):
        acc = carry[2 * hh + 1]
        o_ref[:, hh * Dh:(hh + 1) * Dh] = (acc[:Dh] / acc[Dh:Dh + 1]).T.astype(BF16)


def _moba_attn(tbl, q, k, vt, mask):
    S = q.shape[0]
    H, Dh, BLK = MOBA_HEADS, MOBA_HEAD_DIM, MOBA_BLOCK
    nb = S // BLK
    NH, G = MOBA_HEADS_PER_STEP, MOBA_KV_GROUP
    return pl.pallas_call(
        _moba_kernel,
        grid=(H // NH, nb),
        in_specs=[
            pl.BlockSpec(memory_space=pltpu.SMEM),
            pl.BlockSpec((BLK, NH * Dh), lambda h, i: (i, h)),
            pl.BlockSpec((S, NH * Dh), lambda h, i: (0, h)),
            pl.BlockSpec((NH, nb, VT_ROWS, BLK), lambda h, i: (h, 0, 0, 0)),
            pl.BlockSpec((NH, nb, BLK), lambda h, i: (h, 0, i)),
        ],
        out_specs=pl.BlockSpec((BLK, NH * Dh), lambda h, i: (i, h)),
        out_shape=jax.ShapeDtypeStruct((S, H * Dh), BF16),
        scratch_shapes=([pltpu.VMEM((NH, BLK, BLK), F32), pltpu.VMEM((NH, BLK, BLK), F32)]
                        + [pltpu.VMEM((NH, G * BLK, BLK), F32)] * MOBA_SCORE_BUFFERS),
        compiler_params=_cparams("arbitrary", "arbitrary"),
        name="moba_attn",
    )(tbl, q, k, vt, mask)


def _row(v):
    return v.reshape(1, -1)


def kernel(x, gla_norm, gla_w_in, gla_gk_w1, gla_gk_w2, gla_gk_b, gla_o_norm, gla_w_out,
           kv_norm, kv_w, k_norm_w, moba_norm, moba_w_q, moba_q_norm, moba_w_out, rel_bias,
           ffn_norm, ffn_w_up, ffn_conv_w, ffn_conv_b, ffn_w_down):
    B, S, D = x.shape
    assert B == 1
    depth = ffn_norm.shape[0]
    n_a = gla_norm.shape[0]
    h = x[0]
    HD = MOBA_HEADS * MOBA_HEAD_DIM
    k_bf = vt = kmean = None
    tbl = rel_bias.T

    for layer in range(depth):
        if layer < n_a:
            a = layer
            R = gla_gk_w1.shape[2]
            w1p = jnp.pad(gla_gk_w1[a].astype(BF16), ((0, 0), (0, LANE - R)))
            w2p = jnp.pad(gla_gk_w2[a].astype(BF16), ((0, LANE - R), (0, 0)))
            proj, log_a = _gla_in(h, _row(gla_norm[a]), gla_w_in[a].astype(BF16), w1p, w2p,
                                  _row(gla_gk_b[a]))
            o = _gla_core(proj, log_a, _row(gla_o_norm[a]))
            h = _mm_res(o, gla_w_out[a].astype(BF16), h)
        else:
            b = layer - n_a
            if k_bf is None:
                kv_bf = kv_w.astype(BF16)
                k_bf = _proj_headnorm(h, _row(kv_norm), kv_bf, _row(k_norm_w), HD)
                vt = _vproj(h, _row(kv_norm), kv_bf, HD)
                kmean = _kmean(k_bf)
            q = _proj_headnorm(h, _row(moba_norm[b]), moba_w_q[b].astype(BF16),
                               _row(moba_q_norm[b]), HD,
                               scale=MOBA_HEAD_DIM ** -0.5 * LOG2E)
            mask = _gate_mask(q, kmean)
            o = _moba_attn(tbl, q, k_bf, vt, mask)
            h = _mm_res(o, moba_w_out[b].astype(BF16), h)
        h = _conv_ffn(h, _row(ffn_norm[layer]), ffn_w_up[layer].astype(BF16),
                      ffn_conv_w[layer], _row(ffn_conv_b[layer]), ffn_w_down[layer].astype(BF16))
    return h[None]
```

```python
import functools
import math

import jax
import jax.numpy as jnp
from jax import lax
from jax.experimental import pallas as pl
from jax.experimental.pallas import tpu as pltpu

F32 = jnp.float32
BF16 = jnp.bfloat16

EPS = 1e-6
GLA_HEADS = 4
GLA_CHUNK = 64
GLA_GATE_NORMALIZER = 16.0
MOBA_HEADS = 16
MOBA_HEAD_DIM = 128
MOBA_BLOCK = 256
MOBA_TOPK = 3
REL_BUCKETS = 32
REL_MAX_DISTANCE = 128
CONV_WIDTH = 3

VMEM_LIMIT_BYTES = 56 * 1024 * 1024
LANE = 128
HALO_ROWS = 8
BF16_SUBLANES = 16
MOBA_KV_GROUP = 1
MOBA_SCORE_BUFFERS = 3
MOBA_ROTATIONS = 2
MOBA_HEADS_PER_STEP = 2
LOG2E = math.log2(math.e)
FFN_ROW_STRIP = 64
VT_ROWS = MOBA_HEAD_DIM + BF16_SUBLANES


def _cparams(*sem):
    return pltpu.CompilerParams(dimension_semantics=sem,
                                vmem_limit_bytes=VMEM_LIMIT_BYTES)


def _rmsnorm_rows(x, w):
    ms = jnp.mean(x * x, axis=-1, keepdims=True)
    return x * lax.rsqrt(ms + EPS) * w


def _dot(a, b):
    return jnp.dot(a, b, preferred_element_type=F32)


def _dot_nt(a, b):
    return lax.dot_general(a, b, (((1,), (1,)), ((), ())), preferred_element_type=F32)


def _dot_tn(a, b):
    return lax.dot_general(a, b, (((0,), (0,)), ((), ())), preferred_element_type=F32)


def _gla_in_kernel(x_ref, nw_ref, w_ref, w1_ref, w2_ref, b_ref, proj_ref, la_ref, xn_scr):
    @pl.when(pl.program_id(1) == 0)
    def _():
        xn = _rmsnorm_rows(x_ref[...], nw_ref[...]).astype(BF16)
        xn_scr[...] = xn
        r = _dot(xn, w1_ref[...])
        gk = _dot(r.astype(BF16), w2_ref[...]) + b_ref[...]
        log_sig = jnp.minimum(gk, 0.0) - jnp.log1p(jnp.exp(-jnp.abs(gk)))
        la_ref[...] = log_sig * (1.0 / GLA_GATE_NORMALIZER)

    proj_ref[...] = _dot(xn_scr[...], w_ref[...])


def _gla_in(x, norm_w, w_in, w1p, w2p, gk_b, *, tm=1024):
    S, D = x.shape
    N = w_in.shape[1]
    KD = w2p.shape[1]
    tn = KD
    return pl.pallas_call(
        _gla_in_kernel,
        grid=(S // tm, N // tn),
        in_specs=[
            pl.BlockSpec((tm, D), lambda i, j: (i, 0)),
            pl.BlockSpec((1, D), lambda i, j: (0, 0)),
            pl.BlockSpec((D, tn), lambda i, j: (0, j)),
            pl.BlockSpec((D, LANE), lambda i, j: (0, 0)),
            pl.BlockSpec((LANE, KD), lambda i, j: (0, 0)),
            pl.BlockSpec((1, KD), lambda i, j: (0, 0)),
        ],
        out_specs=[
            pl.BlockSpec((None, tm, tn), lambda i, j: (j, i, 0)),
            pl.BlockSpec((tm, KD), lambda i, j: (i, 0)),
        ],
        out_shape=[jax.ShapeDtypeStruct((N // tn, S, tn), F32),
                   jax.ShapeDtypeStruct((S, KD), F32)],
        scratch_shapes=[pltpu.VMEM((tm, D), BF16)],
        compiler_params=_cparams("arbitrary", "arbitrary"),
        name="gla_in",
    )(x, norm_w, w_in, w1p, w2p, gk_b)


def _gla_core_kernel(q_ref, k_ref, v_ref, g_ref, la_ref, onw_ref, o_ref, st_ref, *, n_chunks):
    C = GLA_CHUNK
    H, dv, dk = st_ref.shape

    @pl.when(pl.program_id(0) == 0)
    def _():
        st_ref[...] = jnp.zeros_like(st_ref)

    row = lax.broadcasted_iota(jnp.int32, (C, C), 0)
    col = lax.broadcasted_iota(jnp.int32, (C, C), 1)
    causal = row >= col
    tril = causal.astype(BF16)
    q_scale = dk ** -0.5

    def chunk(c, carry):
        rows = pl.ds(pl.multiple_of(c * C, C), C)
        for h in range(H):
            kc = slice(h * dk, (h + 1) * dk)
            vc = slice(h * dv, (h + 1) * dv)
            per = v_ref.shape[2] // dv
            vchunk, vcc = h // per, slice((h % per) * dv, (h % per + 1) * dv)
            la = la_ref[rows, kc]
            la_hi = la.astype(BF16)
            la_lo = (la - la_hi.astype(F32)).astype(BF16)
            b = _dot(tril, la_hi) + _dot(tril, la_lo)
            b_last = b[C - 1:C, :]
            q = q_ref[rows, kc]
            k = k_ref[rows, kc]
            v = v_ref[vchunk, rows, vcc].astype(BF16)
            q_dec = ((q * q_scale) * jnp.exp(b)).astype(BF16)
            k_inv = (k * jnp.exp(-b)).astype(BF16)
            k_end = (k * jnp.exp(b_last - b)).astype(BF16)
            chunk_decay = jnp.exp(b_last)
            attn = jnp.where(causal, _dot_nt(q_dec, k_inv), 0.0)
            st = st_ref[h]
            o = _dot(attn.astype(BF16), v) + _dot_nt(q_dec, st.astype(BF16))
            st_ref[h] = st * chunk_decay + _dot_tn(v, k_end)
            y = _rmsnorm_rows(o, onw_ref[...])
            g = g_ref[vchunk, rows, vcc]
            y = y * (g * (1.0 / (1.0 + jnp.exp(-g))))
            o_ref[rows, vc] = y.astype(BF16)
        return carry

    lax.fori_loop(0, n_chunks, chunk, 0, unroll=2)


def _gla_core(proj, log_a, o_norm_w, *, tile=256):
    n_col_chunks, S, KD = proj.shape
    H = GLA_HEADS
    assert log_a.shape[1] == KD
    nv = (n_col_chunks - 2) // 2
    assert 2 % nv == 0 and n_col_chunks == 2 + 2 * nv
    VD = nv * KD
    return pl.pallas_call(
        functools.partial(_gla_core_kernel, n_chunks=tile // GLA_CHUNK),
        grid=(S // tile,),
        in_specs=[
            pl.BlockSpec((None, tile, KD), lambda t: (0, t, 0)),
            pl.BlockSpec((None, tile, KD), lambda t: (1, t, 0)),
            pl.BlockSpec((nv, tile, KD), lambda t: (2 // nv, t, 0)),
            pl.BlockSpec((nv, tile, KD), lambda t: ((2 + nv) // nv, t, 0)),
            pl.BlockSpec((tile, KD), lambda t: (t, 0)),
            pl.BlockSpec((1, VD // H), lambda t: (0, 0)),
        ],
        out_specs=pl.BlockSpec((tile, VD), lambda t: (t, 0)),
        out_shape=jax.ShapeDtypeStruct((S, VD), BF16),
        scratch_shapes=[pltpu.VMEM((H, VD // H, KD // H), F32)],
        compiler_params=_cparams("arbitrary"),
        name="gla_core",
    )(proj, proj, proj, proj, log_a, o_norm_w)


def _mm_res_kernel(a_ref, w_ref, r_ref, o_ref):
    o_ref[...] = r_ref[...] + _dot(a_ref[...], w_ref[...])


def _mm_res(a, w, res, *, tm=512, tn=2048):
    S, K = a.shape
    N = w.shape[1]
    return pl.pallas_call(
        _mm_res_kernel,
        grid=(S // tm, N // tn),
        in_specs=[
            pl.BlockSpec((tm, K), lambda i, j: (i, 0)),
            pl.BlockSpec((K, tn), lambda i, j: (0, j)),
            pl.BlockSpec((tm, tn), lambda i, j: (i, j)),
        ],
        out_specs=pl.BlockSpec((tm, tn), lambda i, j: (i, j)),
        out_shape=jax.ShapeDtypeStruct((S, N), F32),
        compiler_params=_cparams("arbitrary", "arbitrary"),
        name="mm_res",
    )(a, w, res)


def _ffn_kernel(x_ref, nw_ref, wa_ref, wu_ref, cwa_ref, cwu_ref, cba_ref, cbu_ref, wd_ref,
                o_ref, xn_scr, halo_a, halo_u, ha0, hu0, ha1, hu1, act0, act1, *, n_chunks, n_steps):
    s = pl.program_id(0)
    last = n_steps - 1
    ja = jnp.minimum(s, last) % n_chunks
    jb = jnp.clip(s - 1, 0, last) % n_chunks
    jc = jnp.clip(s - 2, 0, last) % n_chunks

    @pl.when(s == 0)
    def _():
        for ref in (halo_a, halo_u, ha0, hu0, ha1, hu1, act0, act1):
            ref[...] = jnp.zeros(ref.shape, ref.dtype)

    @pl.when((ja == 0) & (s <= last))
    def _():
        xn_scr[...] = _rmsnorm_rows(x_ref[...], nw_ref[...]).astype(BF16)

    @pl.when(jc == 0)
    def _():
        o_ref[...] = x_ref[...]

    tm, tf = act0.shape
    top = HALO_ROWS

    n = FFN_ROW_STRIP
    sub = HALO_ROWS

    def taps(cw_ref, cb_ref, cols):
        rep = lambda r: jnp.broadcast_to(r, (sub, LANE))
        return (rep(cw_ref[0:1, cols]), rep(cw_ref[1:2, cols]), rep(cw_ref[2:3, cols]),
                rep(cb_ref[:, cols]))

    def conv_rows(h_ref, w, r0, cols):
        win = lambda off: h_ref[top + off + r0:top + off + r0 + n, cols].reshape(n // sub, sub, LANE)
        return w[3] + w[0] * win(-2) + w[1] * win(-1) + w[2] * win(0)

    def stages(h_out, h_in, act_out, act_in):
        for h_ref, halo_ref in ((h_in[0], halo_a), (h_in[1], halo_u)):
            h_ref[0:top, :] = halo_ref[jb]
            halo_ref[jb] = h_ref[tm:tm + top, :]
        for c0 in range(0, tf, LANE):
            cols = slice(c0, c0 + LANE)
            wa_t = taps(cwa_ref, cba_ref, cols)
            wu_t = taps(cwu_ref, cbu_ref, cols)
            for r0 in range(0, tm, n):
                ca = conv_rows(h_in[0], wa_t, r0, cols)
                cu = conv_rows(h_in[1], wu_t, r0, cols)
                act = (ca * (1.0 / (1.0 + jnp.exp(-ca)))) * cu
                act_out[r0:r0 + n, cols] = act.reshape(n, LANE).astype(BF16)
        xn = xn_scr[...]
        h_out[0][top:top + tm, :] = _dot(xn, wa_ref[...])
        h_out[1][top:top + tm, :] = _dot(xn, wu_ref[...])
        o_ref[...] += _dot(act_in[...], wd_ref[...])

    @pl.when(s % 2 == 0)
    def _():
        stages((ha0, hu0), (ha1, hu1), act0, act1)

    @pl.when(s % 2 == 1)
    def _():
        stages((ha1, hu1), (ha0, hu0), act1, act0)


def _conv_ffn(x, norm_w, w_up, conv_w, conv_b, w_down, *, tm=1024, tf=256):
    S, D = x.shape
    Fh = w_down.shape[0]
    nF = Fh // tf
    assert nF > 2
    T = (S // tm) * nF
    last = T - 1

    def a_step(s):
        return jnp.minimum(s, last)

    def b_chunk(s):
        return jnp.clip(s - 1, 0, last) % nF

    def c_step(s):
        return jnp.clip(s - 2, 0, last)

    return pl.pallas_call(
        functools.partial(_ffn_kernel, n_chunks=nF, n_steps=T),
        grid=(T + 2,),
        in_specs=[
            pl.BlockSpec((tm, D), lambda s: (a_step(s) // nF, 0)),
            pl.BlockSpec((1, D), lambda s: (0, 0)),
            pl.BlockSpec((D, tf), lambda s: (0, a_step(s) % nF)),
            pl.BlockSpec((D, tf), lambda s: (0, nF + a_step(s) % nF)),
            pl.BlockSpec((CONV_WIDTH, tf), lambda s: (0, b_chunk(s))),
            pl.BlockSpec((CONV_WIDTH, tf), lambda s: (0, nF + b_chunk(s))),
            pl.BlockSpec((1, tf), lambda s: (0, b_chunk(s))),
            pl.BlockSpec((1, tf), lambda s: (0, nF + b_chunk(s))),
            pl.BlockSpec((tf, D), lambda s: (c_step(s) % nF, 0)),
        ],
        out_specs=pl.BlockSpec((tm, D), lambda s: (c_step(s) // nF, 0)),
        out_shape=jax.ShapeDtypeStruct((S, D), F32),
        scratch_shapes=[pltpu.VMEM((tm, D), BF16),
                        pltpu.VMEM((nF, HALO_ROWS, tf), F32),
                        pltpu.VMEM((nF, HALO_ROWS, tf), F32),
                        pltpu.VMEM((HALO_ROWS + tm, tf), F32), pltpu.VMEM((HALO_ROWS + tm, tf), F32),
                        pltpu.VMEM((HALO_ROWS + tm, tf), F32), pltpu.VMEM((HALO_ROWS + tm, tf), F32),
                        pltpu.VMEM((tm, tf), BF16), pltpu.VMEM((tm, tf), BF16)],
        compiler_params=_cparams("arbitrary"),
        name="conv_ffn",
    )(x, norm_w, w_up, w_up, conv_w, conv_w, conv_b, conv_b, w_down)


def _proj_headnorm_kernel(x_ref, nw_ref, w_ref, hw_ref, o_ref, xn_scr, *, scale):
    @pl.when(pl.program_id(1) == 0)
    def _():
        xn_scr[...] = _rmsnorm_rows(x_ref[...], nw_ref[...]).astype(BF16)

    y = _dot(xn_scr[...], w_ref[...])
    Dh = hw_ref.shape[1]
    for hh in range(y.shape[1] // Dh):
        cols = slice(hh * Dh, (hh + 1) * Dh)
        yn = _rmsnorm_rows(y[:, cols], hw_ref[...])
        if scale is not None:
            yn = yn * scale
        o_ref[:, cols] = yn.astype(BF16)


def _proj_headnorm(x, norm_w, w, head_w, n_out, *, scale=None, tm=512, tn=2048):
    S, D = x.shape
    Dh = head_w.shape[1]
    return pl.pallas_call(
        functools.partial(_proj_headnorm_kernel, scale=scale),
        grid=(S // tm, n_out // tn),
        in_specs=[
            pl.BlockSpec((tm, D), lambda i, j: (i, 0)),
            pl.BlockSpec((1, D), lambda i, j: (0, 0)),
            pl.BlockSpec((D, tn), lambda i, j: (0, j)),
            pl.BlockSpec((1, Dh), lambda i, j: (0, 0)),
        ],
        out_specs=pl.BlockSpec((tm, tn), lambda i, j: (i, j)),
        out_shape=jax.ShapeDtypeStruct((S, n_out), BF16),
        scratch_shapes=[pltpu.VMEM((tm, D), BF16)],
        compiler_params=_cparams("arbitrary", "arbitrary"),
        name="proj_headnorm",
    )(x, norm_w, w, head_w)


def _vproj_kernel(x_ref, nw_ref, w_ref, vt_ref, xn_scr):
    @pl.when(pl.program_id(1) == 0)
    def _():
        xn_scr[...] = _rmsnorm_rows(x_ref[...], nw_ref[...]).astype(BF16)

    y = _dot(xn_scr[...], w_ref[...])
    n_h, n_b, rows, BLK = vt_ref.shape
    Dh = MOBA_HEAD_DIM
    ones = jnp.ones((rows - Dh, BLK), BF16)
    for hh in range(n_h):
        for bb in range(n_b):
            blk = y[bb * BLK:(bb + 1) * BLK, hh * Dh:(hh + 1) * Dh]
            vt_ref[hh, bb, 0:Dh, :] = blk.T.astype(BF16)
            vt_ref[hh, bb, Dh:rows, :] = ones


def _vproj(x, norm_w, w_kv, col0, *, tm=512, tn=2048):
    S, D = x.shape
    H, Dh, BLK = MOBA_HEADS, MOBA_HEAD_DIM, MOBA_BLOCK
    off = col0 // tn
    return pl.pallas_call(
        _vproj_kernel,
        grid=(S // tm, (H * Dh) // tn),
        in_specs=[
            pl.BlockSpec((tm, D), lambda i, j: (i, 0)),
            pl.BlockSpec((1, D), lambda i, j: (0, 0)),
            pl.BlockSpec((D, tn), lambda i, j: (0, off + j)),
        ],
        out_specs=pl.BlockSpec((tn // Dh, tm // BLK, VT_ROWS, BLK), lambda i, j: (j, i, 0, 0)),
        out_shape=jax.ShapeDtypeStruct((H, S // BLK, VT_ROWS, BLK), BF16),
        scratch_shapes=[pltpu.VMEM((tm, D), BF16)],
        compiler_params=_cparams("arbitrary", "arbitrary"),
        name="v_proj",
    )(x, norm_w, w_kv)


def _kmean_kernel(k_ref, o_ref):
    rows, cols = k_ref.shape
    k = k_ref[...].astype(F32).reshape(rows // MOBA_BLOCK, MOBA_BLOCK, cols)
    o_ref[...] = jnp.sum(k, axis=1) * (1.0 / MOBA_BLOCK)


def _kmean(k, *, tr=2048, tc=512):
    S, N = k.shape
    return pl.pallas_call(
        _kmean_kernel,
        grid=(S // tr, N // tc),
        in_specs=[pl.BlockSpec((tr, tc), lambda i, j: (i, j))],
        out_specs=pl.BlockSpec((tr // MOBA_BLOCK, tc), lambda i, j: (i, j)),
        out_shape=jax.ShapeDtypeStruct((S // MOBA_BLOCK, N), F32),
        compiler_params=_cparams("arbitrary", "arbitrary"),
        name="kmean",
    )(k)


def _gate_kernel(q_ref, km_ref, mask_ref):
    nb = km_ref.shape[0]
    tq = q_ref.shape[0]
    g = _dot_nt(km_ref[...].astype(BF16), q_ref[...])
    blk = lax.broadcasted_iota(jnp.int32, (nb, tq), 0)
    pos = pl.program_id(1) * tq + lax.broadcasted_iota(jnp.int32, (nb, tq), 1)
    own = lax.shift_right_logical(pos, int(math.log2(MOBA_BLOCK)))
    neg = -jnp.inf
    v = jnp.where(blk < own, g, neg)
    sel = jnp.zeros((nb, tq), jnp.bool_)
    for r in range(min(MOBA_TOPK, nb)):
        m = jnp.max(v, axis=0, keepdims=True)
        first = jnp.min(jnp.where(v == m, blk, nb), axis=0, keepdims=True)
        onehot = blk == first
        sel = sel | (onehot & (own > r))
        v = jnp.where(onehot, neg, v)
    mask_ref[0] = jnp.where(sel, 0.0, neg)


def _gate_mask(q, kmean, *, tq=2048):
    S = q.shape[0]
    H, Dh = MOBA_HEADS, MOBA_HEAD_DIM
    nb = kmean.shape[0]
    return pl.pallas_call(
        _gate_kernel,
        grid=(H, S // tq),
        in_specs=[
            pl.BlockSpec((tq, Dh), lambda h, t: (t, h)),
            pl.BlockSpec((nb, Dh), lambda h, t: (0, h)),
        ],
        out_specs=pl.BlockSpec((1, nb, tq), lambda h, t: (h, 0, t)),
        out_shape=jax.ShapeDtypeStruct((H, nb, S), F32),
        compiler_params=_cparams("arbitrary", "arbitrary"),
        name="gate_topk",
    )(q, kmean)


def _rel_bias_tile(tbl_ref, h, dist):
    max_exact = REL_BUCKETS // 2
    n = jnp.maximum(dist, 0)
    large = max_exact + (jnp.log(jnp.maximum(n, 1).astype(F32) / max_exact)
                         / math.log(REL_MAX_DISTANCE / max_exact)
                         * (REL_BUCKETS - max_exact)).astype(jnp.int32)
    large = jnp.minimum(large, REL_BUCKETS - 1)
    bucket = jnp.where(n < max_exact, n, large)
    out = jnp.zeros(dist.shape, F32)
    for b in range(REL_BUCKETS):
        out = jnp.where(bucket == b, tbl_ref[h, b], out)
    return out


def _moba_kernel(tbl_ref, q_ref, k_ref, vt_ref, mask_ref, o_ref,
                 bown_ref, bprev_ref, *score_bufs):
    BLK, G, Dh = MOBA_BLOCK, MOBA_KV_GROUP, MOBA_HEAD_DIM
    NH, nb = vt_ref.shape[0], vt_ref.shape[1]
    NBUF = len(score_bufs)
    PHASES = NBUF * MOBA_ROTATIONS
    STEP = PHASES * G
    head0 = pl.program_id(0) * NH
    i = pl.program_id(1)
    neg = -jnp.inf

    @pl.when(i == 0)
    def _():
        key = lax.broadcasted_iota(jnp.int32, (BLK, BLK), 0)
        qry = lax.broadcasted_iota(jnp.int32, (BLK, BLK), 1)
        dist = qry - key
        for hh in range(NH):
            own = _rel_bias_tile(tbl_ref, head0 + hh, dist) * LOG2E
            bown_ref[hh] = jnp.where(dist >= 0, own, neg)
            bprev_ref[hh] = _rel_bias_tile(tbl_ref, head0 + hh, dist + BLK) * LOG2E

    q = [q_ref[:, hh * Dh:(hh + 1) * Dh] for hh in range(NH)]

    def k_rows(hh, j, n):
        return k_ref[pl.ds(pl.multiple_of(j * BLK, BLK), n * BLK), hh * Dh:(hh + 1) * Dh]

    def colmax(s):
        return jnp.max(s, axis=0, keepdims=True)

    def put_scores(s_ref, j0):
        j0 = jnp.minimum(j0, nb - G)
        for hh in range(NH):
            s_ref[hh] = _dot_nt(k_rows(hh, j0, G), q[hh])

    for b in range(NBUF - 1):
        put_scores(score_bufs[b], b * G)

    jp = jnp.maximum(i - 1, 0)
    carry = []
    for hh in range(NH):
        s_own = _dot_nt(k_rows(hh, i, 1), q[hh]) + bown_ref[hh]
        s_prev = _dot_nt(k_rows(hh, jp, 1), q[hh]) + bprev_ref[hh]
        sel_prev = mask_ref[hh, pl.ds(jp, 1), :]
        m = jnp.maximum(colmax(s_own), colmax(s_prev) + sel_prev)
        p_own = jnp.exp2(s_own - m).astype(BF16)
        p_prev = jnp.exp2(s_prev - (m - sel_prev)).astype(BF16)
        carry += [m, _dot(vt_ref[hh, i], p_own) + _dot(vt_ref[hh, jp], p_prev)]

    far_bias = [tbl_ref[head0 + hh, REL_BUCKETS - 1] * LOG2E for hh in range(NH)]

    def update(s_ref, j0, carry):
        out = []
        for hh in range(NH):
            m, acc = carry[2 * hh], carry[2 * hh + 1]
            sel = []
            m_new = m
            for g in range(G):
                jl = jnp.minimum(j0 + g, nb - 1)
                row = (mask_ref[hh, pl.ds(jl, 1), :]
                       + jnp.where(j0 + g < i - 1, far_bias[hh], neg))
                sel.append(row)
                m_new = jnp.maximum(m_new, colmax(s_ref[hh, g * BLK:(g + 1) * BLK, :]) + row)
            acc = acc * jnp.exp2(m - m_new)
            for g in range(G):
                p = jnp.exp2(s_ref[hh, g * BLK:(g + 1) * BLK, :] - (m_new - sel[g])).astype(BF16)
                acc = acc + _dot(vt_ref[hh, jnp.minimum(j0 + g, nb - 1)], p)
            out += [m_new, acc]
        return out

    def trip(t, carry):
        j0 = t * STEP
        carry = list(carry)
        for ph in range(PHASES):
            put_scores(score_bufs[(ph + NBUF - 1) % NBUF], j0 + (ph + NBUF - 1) * G)
            carry = update(score_bufs[ph % NBUF], j0 + ph * G, carry)
        return tuple(carry)

    n_trips = (i + (STEP - 2)) // STEP
    carry = lax.fori_loop(0, n_trips, trip, tuple(carry))
    for hh in range(NH):
        acc = carry[2 * hh + 1]
        o_ref[:, hh * Dh:(hh + 1) * Dh] = (acc[:Dh] / acc[Dh:Dh + 1]).T.astype(BF16)


def _moba_attn(tbl, q, k, vt, mask):
    S = q.shape[0]
    H, Dh, BLK = MOBA_HEADS, MOBA_HEAD_DIM, MOBA_BLOCK
    nb = S // BLK
    NH, G = MOBA_HEADS_PER_STEP, MOBA_KV_GROUP
    return pl.pallas_call(
        _moba_kernel,
        grid=(H // NH, nb),
        in_specs=[
            pl.BlockSpec(memory_space=pltpu.SMEM),
            pl.BlockSpec((BLK, NH * Dh), lambda h, i: (i, h)),
            pl.BlockSpec((S, NH * Dh), lambda h, i: (0, h)),
            pl.BlockSpec((NH, nb, VT_ROWS, BLK), lambda h, i: (h, 0, 0, 0)),
            pl.BlockSpec((NH, nb, BLK), lambda h, i: (h, 0, i)),
        ],
        out_specs=pl.BlockSpec((BLK, NH * Dh), lambda h, i: (i, h)),
        out_shape=jax.ShapeDtypeStruct((S, H * Dh), BF16),
        scratch_shapes=([pltpu.VMEM((NH, BLK, BLK), F32), pltpu.VMEM((NH, BLK, BLK), F32)]
                        + [pltpu.VMEM((NH, G * BLK, BLK), F32)] * MOBA_SCORE_BUFFERS),
        compiler_params=_cparams("arbitrary", "arbitrary"),
        name="moba_attn",
    )(tbl, q, k, vt, mask)


def _row(v):
    return v.reshape(1, -1)


def kernel(x, gla_norm, gla_w_in, gla_gk_w1, gla_gk_w2, gla_gk_b, gla_o_norm, gla_w_out,
           kv_norm, kv_w, k_norm_w, moba_norm, moba_w_q, moba_q_norm, moba_w_out, rel_bias,
           ffn_norm, ffn_w_up, ffn_conv_w, ffn_conv_b, ffn_w_down):
    B, S, D = x.shape
    assert B == 1
    depth = ffn_norm.shape[0]
    n_a = gla_norm.shape[0]
    h = x[0]
    HD = MOBA_HEADS * MOBA_HEAD_DIM
    k_bf = vt = kmean = None
    tbl = rel_bias.T

    for layer in range(depth):
        if layer < n_a:
            a = layer
            R = gla_gk_w1.shape[2]
            w1p = jnp.pad(gla_gk_w1[a].astype(BF16), ((0, 0), (0, LANE - R)))
            w2p = jnp.pad(gla_gk_w2[a].astype(BF16), ((0, LANE - R), (0, 0)))
            proj, log_a = _gla_in(h, _row(gla_norm[a]), gla_w_in[a].astype(BF16), w1p, w2p,
                                  _row(gla_gk_b[a]))
            o = _gla_core(proj, log_a, _row(gla_o_norm[a]))
            h = _mm_res(o, gla_w_out[a].astype(BF16), h)
        else:
            b = layer - n_a
            if k_bf is None:
                kv_bf = kv_w.astype(BF16)
                k_bf = _proj_headnorm(h, _row(kv_norm), kv_bf, _row(k_norm_w), HD)
                vt = _vproj(h, _row(kv_norm), kv_bf, HD)
                kmean = _kmean(k_bf)
            q = _proj_headnorm(h, _row(moba_norm[b]), moba_w_q[b].astype(BF16),
                               _row(moba_q_norm[b]), HD,
                               scale=MOBA_HEAD_DIM ** -0.5 * LOG2E)
            mask = _gate_mask(q, kmean)
            o = _moba_attn(tbl, q, k_bf, vt, mask)
            h = _mm_res(o, moba_w_out[b].astype(BF16), h)
        h = _conv_ffn(h, _row(ffn_norm[layer]), ffn_w_up[layer].astype(BF16),
                      ffn_conv_w[layer], _row(ffn_conv_b[layer]), ffn_w_down[layer].astype(BF16))
    return h[None]
```

```python
import functools
import math

import jax
import jax.numpy as jnp
from jax import lax
from jax.experimental import pallas as pl
from jax.experimental.pallas import tpu as pltpu

F32 = jnp.float32
BF16 = jnp.bfloat16

EPS = 1e-6
GLA_HEADS = 4
GLA_CHUNK = 64
GLA_GATE_NORMALIZER = 16.0
MOBA_HEADS = 16
MOBA_HEAD_DIM = 128
MOBA_BLOCK = 256
MOBA_TOPK = 3
REL_BUCKETS = 32
REL_MAX_DISTANCE = 128
CONV_WIDTH = 3

VMEM_LIMIT_BYTES = 56 * 1024 * 1024
LANE = 128
HALO_ROWS = 8
BF16_SUBLANES = 16
MOBA_KV_GROUP = 1
MOBA_SCORE_BUFFERS = 3
MOBA_ROTATIONS = 2
MOBA_HEADS_PER_STEP = 2
LOG2E = math.log2(math.e)
FFN_ROW_STRIP = 64
VT_ROWS = MOBA_HEAD_DIM + BF16_SUBLANES


def _cparams(*sem):
    return pltpu.CompilerParams(dimension_semantics=sem,
                                vmem_limit_bytes=VMEM_LIMIT_BYTES)


def _rmsnorm_rows(x, w):
    ms = jnp.mean(x * x, axis=-1, keepdims=True)
    return x * lax.rsqrt(ms + EPS) * w


def _dot(a, b):
    return jnp.dot(a, b, preferred_element_type=F32)


def _dot_nt(a, b):
    return lax.dot_general(a, b, (((1,), (1,)), ((), ())), preferred_element_type=F32)


def _dot_tn(a, b):
    return lax.dot_general(a, b, (((0,), (0,)), ((), ())), preferred_element_type=F32)


def _gla_in_kernel(x_ref, nw_ref, w_ref, w1_ref, w2_ref, b_ref, proj_ref, la_ref, xn_scr):
    @pl.when(pl.program_id(1) == 0)
    def _():
        xn = _rmsnorm_rows(x_ref[...], nw_ref[...]).astype(BF16)
        xn_scr[...] = xn
        r = _dot(xn, w1_ref[...])
        gk = _dot(r.astype(BF16), w2_ref[...]) + b_ref[...]
        log_sig = jnp.minimum(gk, 0.0) - jnp.log1p(jnp.exp(-jnp.abs(gk)))
        la_ref[...] = log_sig * (1.0 / GLA_GATE_NORMALIZER)

    proj_ref[...] = _dot(xn_scr[...], w_ref[...])


def _gla_in(x, norm_w, w_in, w1p, w2p, gk_b, *, tm=1024):
    S, D = x.shape
    N = w_in.shape[1]
    KD = w2p.shape[1]
    tn = KD
    return pl.pallas_call(
        _gla_in_kernel,
        grid=(S // tm, N // tn),
        in_specs=[
            pl.BlockSpec((tm, D), lambda i, j: (i, 0)),
            pl.BlockSpec((1, D), lambda i, j: (0, 0)),
            pl.BlockSpec((D, tn), lambda i, j: (0, j)),
            pl.BlockSpec((D, LANE), lambda i, j: (0, 0)),
            pl.BlockSpec((LANE, KD), lambda i, j: (0, 0)),
            pl.BlockSpec((1, KD), lambda i, j: (0, 0)),
        ],
        out_specs=[
            pl.BlockSpec((None, tm, tn), lambda i, j: (j, i, 0)),
            pl.BlockSpec((tm, KD), lambda i, j: (i, 0)),
        ],
        out_shape=[jax.ShapeDtypeStruct((N // tn, S, tn), F32),
                   jax.ShapeDtypeStruct((S, KD), F32)],
        scratch_shapes=[pltpu.VMEM((tm, D), BF16)],
        compiler_params=_cparams("arbitrary", "arbitrary"),
        name="gla_in",
    )(x, norm_w, w_in, w1p, w2p, gk_b)


def _gla_core_kernel(q_ref, k_ref, v_ref, g_ref, la_ref, onw_ref, o_ref, st_ref, *, n_chunks):
    C = GLA_CHUNK
    H, dv, dk = st_ref.shape

    @pl.when(pl.program_id(0) == 0)
    def _():
        st_ref[...] = jnp.zeros_like(st_ref)

    row = lax.broadcasted_iota(jnp.int32, (C, C), 0)
    col = lax.broadcasted_iota(jnp.int32, (C, C), 1)
    causal = row >= col
    tril = causal.astype(BF16)
    q_scale = dk ** -0.5

    def chunk(c, carry):
        rows = pl.ds(pl.multiple_of(c * C, C), C)
        for h in range(H):
            kc = slice(h * dk, (h + 1) * dk)
            vc = slice(h * dv, (h + 1) * dv)
            per = v_ref.shape[2] // dv
            vchunk, vcc = h // per, slice((h % per) * dv, (h % per + 1) * dv)
            la = la_ref[rows, kc]
            la_hi = la.astype(BF16)
            la_lo = (la - la_hi.astype(F32)).astype(BF16)
            b = _dot(tril, la_hi) + _dot(tril, la_lo)
            b_last = b[C - 1:C, :]
            q = q_ref[rows, kc]
            k = k_ref[rows, kc]
            v = v_ref[vchunk, rows, vcc].astype(BF16)
            q_dec = ((q * q_scale) * jnp.exp(b)).astype(BF16)
            k_inv = (k * jnp.exp(-b)).astype(BF16)
            k_end = (k * jnp.exp(b_last - b)).astype(BF16)
            chunk_decay = jnp.exp(b_last)
            attn = jnp.where(causal, _dot_nt(q_dec, k_inv), 0.0)
            st = st_ref[h]
            o = _dot(attn.astype(BF16), v) + _dot_nt(q_dec, st.astype(BF16))
            st_ref[h] = st * chunk_decay + _dot_tn(v, k_end)
            y = _rmsnorm_rows(o, onw_ref[...])
            g = g_ref[vchunk, rows, vcc]
            y = y * (g * (1.0 / (1.0 + jnp.exp(-g))))
            o_ref[rows, vc] = y.astype(BF16)
        return carry

    lax.fori_loop(0, n_chunks, chunk, 0, unroll=2)


def _gla_core(proj, log_a, o_norm_w, *, tile=256):
    n_col_chunks, S, KD = proj.shape
    H = GLA_HEADS
    assert log_a.shape[1] == KD
    nv = (n_col_chunks - 2) // 2
    assert 2 % nv == 0 and n_col_chunks == 2 + 2 * nv
    VD = nv * KD
    return pl.pallas_call(
        functools.partial(_gla_core_kernel, n_chunks=tile // GLA_CHUNK),
        grid=(S // tile,),
        in_specs=[
            pl.BlockSpec((None, tile, KD), lambda t: (0, t, 0)),
            pl.BlockSpec((None, tile, KD), lambda t: (1, t, 0)),
            pl.BlockSpec((nv, tile, KD), lambda t: (2 // nv, t, 0)),
            pl.BlockSpec((nv, tile, KD), lambda t: ((2 + nv) // nv, t, 0)),
            pl.BlockSpec((tile, KD), lambda t: (t, 0)),
            pl.BlockSpec((1, VD // H), lambda t: (0, 0)),
        ],
        out_specs=pl.BlockSpec((tile, VD), lambda t: (t, 0)),
        out_shape=jax.ShapeDtypeStruct((S, VD), BF16),
        scratch_shapes=[pltpu.VMEM((H, VD // H, KD // H), F32)],
        compiler_params=_cparams("arbitrary"),
        name="gla_core",
    )(proj, proj, proj, proj, log_a, o_norm_w)


def _mm_res_kernel(a_ref, w_ref, r_ref, o_ref):
    o_ref[...] = r_ref[...] + _dot(a_ref[...], w_ref[...])


def _mm_res(a, w, res, *, tm=512, tn=2048):
    S, K = a.shape
    N = w.shape[1]
    return pl.pallas_call(
        _mm_res_kernel,
        grid=(S // tm, N // tn),
        in_specs=[
            pl.BlockSpec((tm, K), lambda i, j: (i, 0)),
            pl.BlockSpec((K, tn), lambda i, j: (0, j)),
            pl.BlockSpec((tm, tn), lambda i, j: (i, j)),
        ],
        out_specs=pl.BlockSpec((tm, tn), lambda i, j: (i, j)),
        out_shape=jax.ShapeDtypeStruct((S, N), F32),
        compiler_params=_cparams("arbitrary", "arbitrary"),
        name="mm_res",
    )(a, w, res)


def _ffn_kernel(x_ref, nw_ref, wa_ref, wu_ref, cwa_ref, cwu_ref, cba_ref, cbu_ref, wd_ref,
                o_ref, xn_scr, halo_a, halo_u, ha0, hu0, ha1, hu1, act0, act1, *, n_chunks, n_steps):
    s = pl.program_id(0)
    last = n_steps - 1
    ja = jnp.minimum(s, last) % n_chunks
    jb = jnp.clip(s - 1, 0, last) % n_chunks
    jc = jnp.clip(s - 2, 0, last) % n_chunks

    @pl.when(s == 0)
    def _():
        for ref in (halo_a, halo_u, ha0, hu0, ha1, hu1, act0, act1):
            ref[...] = jnp.zeros(ref.shape, ref.dtype)

    @pl.when((ja == 0) & (s <= last))
    def _():
        xn_scr[...] = _rmsnorm_rows(x_ref[...], nw_ref[...]).astype(BF16)

    @pl.when(jc == 0)
    def _():
        o_ref[...] = x_ref[...]

    tm, tf = act0.shape
    top = HALO_ROWS

    n = FFN_ROW_STRIP
    sub = HALO_ROWS

    def taps(cw_ref, cb_ref, cols):
        rep = lambda r: jnp.broadcast_to(r, (sub, LANE))
        return (rep(cw_ref[0:1, cols]), rep(cw_ref[1:2, cols]), rep(cw_ref[2:3, cols]),
                rep(cb_ref[:, cols]))

    def conv_rows(h_ref, w, r0, cols):
        win = lambda off: h_ref[top + off + r0:top + off + r0 + n, cols].reshape(n // sub, sub, LANE)
        return w[3] + w[0] * win(-2) + w[1] * win(-1) + w[2] * win(0)

    def stages(h_out, h_in, act_out, act_in):
        for h_ref, halo_ref in ((h_in[0], halo_a), (h_in[1], halo_u)):
            h_ref[0:top, :] = halo_ref[jb]
            halo_ref[jb] = h_ref[tm:tm + top, :]
        for c0 in range(0, tf, LANE):
            cols = slice(c0, c0 + LANE)
            wa_t = taps(cwa_ref, cba_ref, cols)
            wu_t = taps(cwu_ref, cbu_ref, cols)
            for r0 in range(0, tm, n):
                ca = conv_rows(h_in[0], wa_t, r0, cols)
                cu = conv_rows(h_in[1], wu_t, r0, cols)
                act = (ca * (1.0 / (1.0 + jnp.exp(-ca)))) * cu
                act_out[r0:r0 + n, cols] = act.reshape(n, LANE).astype(BF16)
        xn = xn_scr[...]
        h_out[0][top:top + tm, :] = _dot(xn, wa_ref[...])
        h_out[1][top:top + tm, :] = _dot(xn, wu_ref[...])
        o_ref[...] += _dot(act_in[...], wd_ref[...])

    @pl.when(s % 2 == 0)
    def _():
        stages((ha0, hu0), (ha1, hu1), act0, act1)

    @pl.when(s % 2 == 1)
    def _():
        stages((ha1, hu1), (ha0, hu0), act1, act0)


def _conv_ffn(x, norm_w, w_up, conv_w, conv_b, w_down, *, tm=1024, tf=256):
    S, D = x.shape
    Fh = w_down.shape[0]
    nF = Fh // tf
    assert nF > 2
    T = (S // tm) * nF
    last = T - 1

    def a_step(s):
        return jnp.minimum(s, last)

    def b_chunk(s):
        return jnp.clip(s - 1, 0, last) % nF

    def c_step(s):
        return jnp.clip(s - 2, 0, last)

    return pl.pallas_call(
        functools.partial(_ffn_kernel, n_chunks=nF, n_steps=T),
        grid=(T + 2,),
        in_specs=[
            pl.BlockSpec((tm, D), lambda s: (a_step(s) // nF, 0)),
            pl.BlockSpec((1, D), lambda s: (0, 0)),
            pl.BlockSpec((D, tf), lambda s: (0, a_step(s) % nF)),
            pl.BlockSpec((D, tf), lambda s: (0, nF + a_step(s) % nF)),
            pl.BlockSpec((CONV_WIDTH, tf), lambda s: (0, b_chunk(s))),
            pl.BlockSpec((CONV_WIDTH, tf), lambda s: (0, nF + b_chunk(s))),
            pl.BlockSpec((1, tf), lambda s: (0, b_chunk(s))),
            pl.BlockSpec((1, tf), lambda s: (0, nF + b_chunk(s))),
            pl.BlockSpec((tf, D), lambda s: (c_step(s) % nF, 0)),
        ],
        out_specs=pl.BlockSpec((tm, D), lambda s: (c_step(s) // nF, 0)),
        out_shape=jax.ShapeDtypeStruct((S, D), F32),
        scratch_shapes=[pltpu.VMEM((tm, D), BF16),
                        pltpu.VMEM((nF, HALO_ROWS, tf), F32),
                        pltpu.VMEM((nF, HALO_ROWS, tf), F32),
                        pltpu.VMEM((HALO_ROWS + tm, tf), F32), pltpu.VMEM((HALO_ROWS + tm, tf), F32),
                        pltpu.VMEM((HALO_ROWS + tm, tf), F32), pltpu.VMEM((HALO_ROWS + tm, tf), F32),
                        pltpu.VMEM((tm, tf), BF16), pltpu.VMEM((tm, tf), BF16)],
        compiler_params=_cparams("arbitrary"),
        name="conv_ffn",
    )(x, norm_w, w_up, w_up, conv_w, conv_w, conv_b, conv_b, w_down)


def _proj_headnorm_kernel(x_ref, nw_ref, w_ref, hw_ref, o_ref, xn_scr, *, scale):
    @pl.when(pl.program_id(1) == 0)
    def _():
        xn_scr[...] = _rmsnorm_rows(x_ref[...], nw_ref[...]).astype(BF16)

    y = _dot(xn_scr[...], w_ref[...])
    Dh = hw_ref.shape[1]
    for hh in range(y.shape[1] // Dh):
        cols = slice(hh * Dh, (hh + 1) * Dh)
        yn = _rmsnorm_rows(y[:, cols], hw_ref[...])
        if scale is not None:
            yn = yn * scale
        o_ref[:, cols] = yn.astype(BF16)


def _proj_headnorm(x, norm_w, w, head_w, n_out, *, scale=None, tm=512, tn=2048):
    S, D = x.shape
    Dh = head_w.shape[1]
    return pl.pallas_call(
        functools.partial(_proj_headnorm_kernel, scale=scale),
        grid=(S // tm, n_out // tn),
        in_specs=[
            pl.BlockSpec((tm, D), lambda i, j: (i, 0)),
            pl.BlockSpec((1, D), lambda i, j: (0, 0)),
            pl.BlockSpec((D, tn), lambda i, j: (0, j)),
            pl.BlockSpec((1, Dh), lambda i, j: (0, 0)),
        ],
        out_specs=pl.BlockSpec((tm, tn), lambda i, j: (i, j)),
        out_shape=jax.ShapeDtypeStruct((S, n_out), BF16),
        scratch_shapes=[pltpu.VMEM((tm, D), BF16)],
        compiler_params=_cparams("arbitrary", "arbitrary"),
        name="proj_headnorm",
    )(x, norm_w, w, head_w)


def _vproj_kernel(x_ref, nw_ref, w_ref, vt_ref, xn_scr):
    @pl.when(pl.program_id(1) == 0)
    def _():
        xn_scr[...] = _rmsnorm_rows(x_ref[...], nw_ref[...]).astype(BF16)

    y = _dot(xn_scr[...], w_ref[...])
    n_h, n_b, rows, BLK = vt_ref.shape
    Dh = MOBA_HEAD_DIM
    ones = jnp.ones((rows - Dh, BLK), BF16)
    for hh in range(n_h):
        for bb in range(n_b):
            blk = y[bb * BLK:(bb + 1) * BLK, hh * Dh:(hh + 1) * Dh]
            vt_ref[hh, bb, 0:Dh, :] = blk.T.astype(BF16)
            vt_ref[hh, bb, Dh:rows, :] = ones


def _vproj(x, norm_w, w_kv, col0, *, tm=512, tn=2048):
    S, D = x.shape
    H, Dh, BLK = MOBA_HEADS, MOBA_HEAD_DIM, MOBA_BLOCK
    off = col0 // tn
    return pl.pallas_call(
        _vproj_kernel,
        grid=(S // tm, (H * Dh) // tn),
        in_specs=[
            pl.BlockSpec((tm, D), lambda i, j: (i, 0)),
            pl.BlockSpec((1, D), lambda i, j: (0, 0)),
            pl.BlockSpec((D, tn), lambda i, j: (0, off + j)),
        ],
        out_specs=pl.BlockSpec((tn // Dh, tm // BLK, VT_ROWS, BLK), lambda i, j: (j, i, 0, 0)),
        out_shape=jax.ShapeDtypeStruct((H, S // BLK, VT_ROWS, BLK), BF16),
        scratch_shapes=[pltpu.VMEM((tm, D), BF16)],
        compiler_params=_cparams("arbitrary", "arbitrary"),
        name="v_proj",
    )(x, norm_w, w_kv)


def _kmean_kernel(k_ref, o_ref):
    rows, cols = k_ref.shape
    k = k_ref[...].astype(F32).reshape(rows // MOBA_BLOCK, MOBA_BLOCK, cols)
    o_ref[...] = jnp.sum(k, axis=1) * (1.0 / MOBA_BLOCK)


def _kmean(k, *, tr=2048, tc=512):
    S, N = k.shape
    return pl.pallas_call(
        _kmean_kernel,
        grid=(S // tr, N // tc),
        in_specs=[pl.BlockSpec((tr, tc), lambda i, j: (i, j))],
        out_specs=pl.BlockSpec((tr // MOBA_BLOCK, tc), lambda i, j: (i, j)),
        out_shape=jax.ShapeDtypeStruct((S // MOBA_BLOCK, N), F32),
        compiler_params=_cparams("arbitrary", "arbitrary"),
        name="kmean",
    )(k)


def _gate_kernel(q_ref, km_ref, mask_ref):
    nb = km_ref.shape[0]
    tq = q_ref.shape[0]
    g = _dot_nt(km_ref[...].astype(BF16), q_ref[...])
    blk = lax.broadcasted_iota(jnp.int32, (nb, tq), 0)
    pos = pl.program_id(1) * tq + lax.broadcasted_iota(jnp.int32, (nb, tq), 1)
    own = lax.shift_right_logical(pos, int(math.log2(MOBA_BLOCK)))
    neg = -jnp.inf
    v = jnp.where(blk < own, g, neg)
    sel = jnp.zeros((nb, tq), jnp.bool_)
    for r in range(min(MOBA_TOPK, nb)):
        m = jnp.max(v, axis=0, keepdims=True)
        first = jnp.min(jnp.where(v == m, blk, nb), axis=0, keepdims=True)
        onehot = blk == first
        sel = sel | (onehot & (own > r))
        v = jnp.where(onehot, neg, v)
    mask_ref[0] = jnp.where(sel, 0.0, neg)


def _gate_mask(q, kmean, *, tq=2048):
    S = q.shape[0]
    H, Dh = MOBA_HEADS, MOBA_HEAD_DIM
    nb = kmean.shape[0]
    return pl.pallas_call(
        _gate_kernel,
        grid=(H, S // tq),
        in_specs=[
            pl.BlockSpec((tq, Dh), lambda h, t: (t, h)),
            pl.BlockSpec((nb, Dh), lambda h, t: (0, h)),
        ],
        out_specs=pl.BlockSpec((1, nb, tq), lambda h, t: (h, 0, t)),
        out_shape=jax.ShapeDtypeStruct((H, nb, S), F32),
        compiler_params=_cparams("arbitrary", "arbitrary"),
        name="gate_topk",
    )(q, kmean)


def _rel_bias_tile(tbl_ref, h, dist):
    max_exact = REL_BUCKETS // 2
    n = jnp.maximum(dist, 0)
    large = max_exact + (jnp.log(jnp.maximum(n, 1).astype(F32) / max_exact)
                         / math.log(REL_MAX_DISTANCE / max_exact)
                         * (REL_BUCKETS - max_exact)).astype(jnp.int32)
    large = jnp.minimum(large, REL_BUCKETS - 1)
    bucket = jnp.where(n < max_exact, n, large)
    out = jnp.zeros(dist.shape, F32)
    for b in range(REL_BUCKETS):
        out = jnp.where(bucket == b, tbl_ref[h, b], out)
    return out


def _moba_kernel(tbl_ref, q_ref, k_ref, vt_ref, mask_ref, o_ref,
                 bown_ref, bprev_ref, *bufs):
    BLK, G, Dh = MOBA_BLOCK, MOBA_KV_GROUP, MOBA_HEAD_DIM
    NH, nb = vt_ref.shape[0], vt_ref.shape[1]
    NBUF = len(bufs) // 2
    score_bufs = list(zip(bufs[:NBUF], bufs[NBUF:]))
    PHASES = NBUF * MOBA_ROTATIONS
    STEP = PHASES * G
    head0 = pl.program_id(0) * NH
    i = pl.program_id(1)
    neg = -jnp.inf

    @pl.when(i == 0)
    def _():
        key = lax.broadcasted_iota(jnp.int32, (BLK, BLK), 0)
        qry = lax.broadcasted_iota(jnp.int32, (BLK, BLK), 1)
        dist = qry - key
        for hh in range(NH):
            own = _rel_bias_tile(tbl_ref, head0 + hh, dist) * LOG2E
            bown_ref[hh] = jnp.where(dist >= 0, own, neg)
            bprev_ref[hh] = _rel_bias_tile(tbl_ref, head0 + hh, dist + BLK) * LOG2E

    q = [q_ref[:, hh * Dh:(hh + 1) * Dh] for hh in range(NH)]

    def k_rows(hh, j, n):
        return k_ref[pl.ds(pl.multiple_of(j * BLK, BLK), n * BLK), hh * Dh:(hh + 1) * Dh]

    def colmax(s):
        return jnp.max(s, axis=0, keepdims=True)

    def put_scores(buf, j0):
        s_ref, mx_ref = buf
        j0 = jnp.minimum(j0, nb - G)
        for hh in range(NH):
            s = _dot_nt(k_rows(hh, j0, G), q[hh])
            s_ref[hh] = s
            for g in range(G):
                mx_ref[hh, g] = colmax(s[g * BLK:(g + 1) * BLK])

    for b in range(NBUF - 1):
        put_scores(score_bufs[b], b * G)

    jp = jnp.maximum(i - 1, 0)
    carry = []
    for hh in range(NH):
        s_own = _dot_nt(k_rows(hh, i, 1), q[hh]) + bown_ref[hh]
        s_prev = _dot_nt(k_rows(hh, jp, 1), q[hh]) + bprev_ref[hh]
        sel_prev = mask_ref[hh, pl.ds(jp, 1), :]
        m = jnp.maximum(colmax(s_own), colmax(s_prev) + sel_prev)
        p_own = jnp.exp2(s_own - m).astype(BF16)
        p_prev = jnp.exp2(s_prev - (m - sel_prev)).astype(BF16)
        carry += [m, _dot(vt_ref[hh, i], p_own) + _dot(vt_ref[hh, jp], p_prev)]

    far_bias = [tbl_ref[head0 + hh, REL_BUCKETS - 1] * LOG2E for hh in range(NH)]

    def update(buf, j0, carry):
        s_ref, mx_ref = buf
        out = []
        for hh in range(NH):
            m, acc = carry[2 * hh], carry[2 * hh + 1]
            sel = []
            m_new = m
            for g in range(G):
                jl = jnp.minimum(j0 + g, nb - 1)
                row = (mask_ref[hh, pl.ds(jl, 1), :]
                       + jnp.where(j0 + g < i - 1, far_bias[hh], neg))
                sel.append(row)
                m_new = jnp.maximum(m_new, mx_ref[hh, g] + row)
            acc = acc * jnp.exp2(m - m_new)
            for g in range(G):
                p = jnp.exp2(s_ref[hh, g * BLK:(g + 1) * BLK, :] - (m_new - sel[g])).astype(BF16)
                acc = acc + _dot(vt_ref[hh, jnp.minimum(j0 + g, nb - 1)], p)
            out += [m_new, acc]
        return out

    def trip(t, carry):
        j0 = t * STEP
        carry = list(carry)
        for ph in range(PHASES):
            put_scores(score_bufs[(ph + NBUF - 1) % NBUF], j0 + (ph + NBUF - 1) * G)
            carry = update(score_bufs[ph % NBUF], j0 + ph * G, carry)
        return tuple(carry)

    n_trips = (i + (STEP - 2)) // STEP
    carry = lax.fori_loop(0, n_trips, trip, tuple(carry))
    for hh in range(NH):
        acc = carry[2 * hh + 1]
        o_ref[:, hh * Dh:(hh + 1) * Dh] = (acc[:Dh] / acc[Dh:Dh + 1]).T.astype(BF16)


def _moba_attn(tbl, q, k, vt, mask):
    S = q.shape[0]
    H, Dh, BLK = MOBA_HEADS, MOBA_HEAD_DIM, MOBA_BLOCK
    nb = S // BLK
    NH, G = MOBA_HEADS_PER_STEP, MOBA_KV_GROUP
    return pl.pallas_call(
        _moba_kernel,
        grid=(H // NH, nb),
        in_specs=[
            pl.BlockSpec(memory_space=pltpu.SMEM),
            pl.BlockSpec((BLK, NH * Dh), lambda h, i: (i, h)),
            pl.BlockSpec((S, NH * Dh), lambda h, i: (0, h)),
            pl.BlockSpec((NH, nb, VT_ROWS, BLK), lambda h, i: (h, 0, 0, 0)),
            pl.BlockSpec((NH, nb, BLK), lambda h, i: (h, 0, i)),
        ],
        out_specs=pl.BlockSpec((BLK, NH * Dh), lambda h, i: (i, h)),
        out_shape=jax.ShapeDtypeStruct((S, H * Dh), BF16),
        scratch_shapes=([pltpu.VMEM((NH, BLK, BLK), F32), pltpu.VMEM((NH, BLK, BLK), F32)]
                        + [pltpu.VMEM((NH, G * BLK, BLK), F32)] * MOBA_SCORE_BUFFERS
                        + [pltpu.VMEM((NH, G, 1, BLK), F32)] * MOBA_SCORE_BUFFERS),
        compiler_params=_cparams("arbitrary", "arbitrary"),
        name="moba_attn",
    )(tbl, q, k, vt, mask)


def _row(v):
    return v.reshape(1, -1)


def kernel(x, gla_norm, gla_w_in, gla_gk_w1, gla_gk_w2, gla_gk_b, gla_o_norm, gla_w_out,
           kv_norm, kv_w, k_norm_w, moba_norm, moba_w_q, moba_q_norm, moba_w_out, rel_bias,
           ffn_norm, ffn_w_up, ffn_conv_w, ffn_conv_b, ffn_w_down):
    B, S, D = x.shape
    assert B == 1
    depth = ffn_norm.shape[0]
    n_a = gla_norm.shape[0]
    h = x[0]
    HD = MOBA_HEADS * MOBA_HEAD_DIM
    k_bf = vt = kmean = None
    tbl = rel_bias.T

    for layer in range(depth):
        if layer < n_a:
            a = layer
            R = gla_gk_w1.shape[2]
            w1p = jnp.pad(gla_gk_w1[a].astype(BF16), ((0, 0), (0, LANE - R)))
            w2p = jnp.pad(gla_gk_w2[a].astype(BF16), ((0, LANE - R), (0, 0)))
            proj, log_a = _gla_in(h, _row(gla_norm[a]), gla_w_in[a].astype(BF16), w1p, w2p,
                                  _row(gla_gk_b[a]))
            o = _gla_core(proj, log_a, _row(gla_o_norm[a]))
            h = _mm_res(o, gla_w_out[a].astype(BF16), h)
        else:
            b = layer - n_a
            if k_bf is None:
                kv_bf = kv_w.astype(BF16)
                k_bf = _proj_headnorm(h, _row(kv_norm), kv_bf, _row(k_norm_w), HD)
                vt = _vproj(h, _row(kv_norm), kv_bf, HD)
                kmean = _kmean(k_bf)
            q = _proj_headnorm(h, _row(moba_norm[b]), moba_w_q[b].astype(BF16),
                               _row(moba_q_norm[b]), HD,
                               scale=MOBA_HEAD_DIM ** -0.5 * LOG2E)
            mask = _gate_mask(q, kmean)
            o = _moba_attn(tbl, q, k_bf, vt, mask)
            h = _mm_res(o, moba_w_out[b].astype(BF16), h)
        h = _conv_ffn(h, _row(ffn_norm[layer]), ffn_w_up[layer].astype(BF16),
                      ffn_conv_w[layer], _row(ffn_conv_b[layer]), ffn_w_down[layer].astype(BF16))
    return h[None]
```

```python
import functools
import math

import jax
import jax.numpy as jnp
from jax import lax
from jax.experimental import pallas as pl
from jax.experimental.pallas import tpu as pltpu

F32 = jnp.float32
BF16 = jnp.bfloat16

EPS = 1e-6
GLA_HEADS = 4
GLA_CHUNK = 64
GLA_GATE_NORMALIZER = 16.0
MOBA_HEADS = 16
MOBA_HEAD_DIM = 128
MOBA_BLOCK = 256
MOBA_TOPK = 3
REL_BUCKETS = 32
REL_MAX_DISTANCE = 128
CONV_WIDTH = 3

VMEM_LIMIT_BYTES = 56 * 1024 * 1024
LANE = 128
HALO_ROWS = 8
BF16_SUBLANES = 16
MOBA_KV_GROUP = 1
MOBA_SCORE_BUFFERS = 3
MOBA_ROTATIONS = 2
MOBA_HEADS_PER_STEP = 2
LOG2E = math.log2(math.e)
VT_ROWS = MOBA_HEAD_DIM + BF16_SUBLANES


def _cparams(*sem):
    return pltpu.CompilerParams(dimension_semantics=sem,
                                vmem_limit_bytes=VMEM_LIMIT_BYTES)


def _rmsnorm_rows(x, w):
    ms = jnp.mean(x * x, axis=-1, keepdims=True)
    return x * lax.rsqrt(ms + EPS) * w


def _dot(a, b):
    return jnp.dot(a, b, preferred_element_type=F32)


def _dot_nt(a, b):
    return lax.dot_general(a, b, (((1,), (1,)), ((), ())), preferred_element_type=F32)


def _dot_tn(a, b):
    return lax.dot_general(a, b, (((0,), (0,)), ((), ())), preferred_element_type=F32)


def _gla_in_kernel(x_ref, nw_ref, w_ref, w1_ref, w2_ref, b_ref, proj_ref, la_ref, xn_scr):
    @pl.when(pl.program_id(1) == 0)
    def _():
        xn = _rmsnorm_rows(x_ref[...], nw_ref[...]).astype(BF16)
        xn_scr[...] = xn
        r = _dot(xn, w1_ref[...])
        gk = _dot(r.astype(BF16), w2_ref[...]) + b_ref[...]
        log_sig = jnp.minimum(gk, 0.0) - jnp.log1p(jnp.exp(-jnp.abs(gk)))
        la_ref[...] = log_sig * (1.0 / GLA_GATE_NORMALIZER)

    proj_ref[...] = _dot(xn_scr[...], w_ref[...])


def _gla_in(x, norm_w, w_in, w1p, w2p, gk_b, *, tm=1024):
    S, D = x.shape
    N = w_in.shape[1]
    KD = w2p.shape[1]
    tn = KD
    return pl.pallas_call(
        _gla_in_kernel,
        grid=(S // tm, N // tn),
        in_specs=[
            pl.BlockSpec((tm, D), lambda i, j: (i, 0)),
            pl.BlockSpec((1, D), lambda i, j: (0, 0)),
            pl.BlockSpec((D, tn), lambda i, j: (0, j)),
            pl.BlockSpec((D, LANE), lambda i, j: (0, 0)),
            pl.BlockSpec((LANE, KD), lambda i, j: (0, 0)),
            pl.BlockSpec((1, KD), lambda i, j: (0, 0)),
        ],
        out_specs=[
            pl.BlockSpec((None, tm, tn), lambda i, j: (j, i, 0)),
            pl.BlockSpec((tm, KD), lambda i, j: (i, 0)),
        ],
        out_shape=[jax.ShapeDtypeStruct((N // tn, S, tn), F32),
                   jax.ShapeDtypeStruct((S, KD), F32)],
        scratch_shapes=[pltpu.VMEM((tm, D), BF16)],
        compiler_params=_cparams("arbitrary", "arbitrary"),
        name="gla_in",
    )(x, norm_w, w_in, w1p, w2p, gk_b)


def _gla_core_kernel(q_ref, k_ref, v_ref, g_ref, la_ref, onw_ref, o_ref, st_ref, *, n_chunks):
    C = GLA_CHUNK
    H, dv, dk = st_ref.shape

    @pl.when(pl.program_id(0) == 0)
    def _():
        st_ref[...] = jnp.zeros_like(st_ref)

    row = lax.broadcasted_iota(jnp.int32, (C, C), 0)
    col = lax.broadcasted_iota(jnp.int32, (C, C), 1)
    causal = row >= col
    tril = causal.astype(BF16)
    q_scale = dk ** -0.5

    def chunk(c, carry):
        rows = pl.ds(pl.multiple_of(c * C, C), C)
        for h in range(H):
            kc = slice(h * dk, (h + 1) * dk)
            vc = slice(h * dv, (h + 1) * dv)
            per = v_ref.shape[2] // dv
            vchunk, vcc = h // per, slice((h % per) * dv, (h % per + 1) * dv)
            la = la_ref[rows, kc]
            la_hi = la.astype(BF16)
            la_lo = (la - la_hi.astype(F32)).astype(BF16)
            b = _dot(tril, la_hi) + _dot(tril, la_lo)
            b_last = b[C - 1:C, :]
            q = q_ref[rows, kc]
            k = k_ref[rows, kc]
            v = v_ref[vchunk, rows, vcc].astype(BF16)
            q_dec = ((q * q_scale) * jnp.exp(b)).astype(BF16)
            k_inv = (k * jnp.exp(-b)).astype(BF16)
            k_end = (k * jnp.exp(b_last - b)).astype(BF16)
            chunk_decay = jnp.exp(b_last)
            attn = jnp.where(causal, _dot_nt(q_dec, k_inv), 0.0)
            st = st_ref[h]
            o = _dot(attn.astype(BF16), v) + _dot_nt(q_dec, st.astype(BF16))
            st_ref[h] = st * chunk_decay + _dot_tn(v, k_end)
            y = _rmsnorm_rows(o, onw_ref[...])
            g = g_ref[vchunk, rows, vcc]
            y = y * (g * (1.0 / (1.0 + jnp.exp(-g))))
            o_ref[rows, vc] = y.astype(BF16)
        return carry

    lax.fori_loop(0, n_chunks, chunk, 0, unroll=2)


def _gla_core(proj, log_a, o_norm_w, *, tile=256):
    n_col_chunks, S, KD = proj.shape
    H = GLA_HEADS
    assert log_a.shape[1] == KD
    nv = (n_col_chunks - 2) // 2
    assert 2 % nv == 0 and n_col_chunks == 2 + 2 * nv
    VD = nv * KD
    return pl.pallas_call(
        functools.partial(_gla_core_kernel, n_chunks=tile // GLA_CHUNK),
        grid=(S // tile,),
        in_specs=[
            pl.BlockSpec((None, tile, KD), lambda t: (0, t, 0)),
            pl.BlockSpec((None, tile, KD), lambda t: (1, t, 0)),
            pl.BlockSpec((nv, tile, KD), lambda t: (2 // nv, t, 0)),
            pl.BlockSpec((nv, tile, KD), lambda t: ((2 + nv) // nv, t, 0)),
            pl.BlockSpec((tile, KD), lambda t: (t, 0)),
            pl.BlockSpec((1, VD // H), lambda t: (0, 0)),
        ],
        out_specs=pl.BlockSpec((tile, VD), lambda t: (t, 0)),
        out_shape=jax.ShapeDtypeStruct((S, VD), BF16),
        scratch_shapes=[pltpu.VMEM((H, VD // H, KD // H), F32)],
        compiler_params=_cparams("arbitrary"),
        name="gla_core",
    )(proj, proj, proj, proj, log_a, o_norm_w)


def _mm_res_kernel(a_ref, w_ref, r_ref, o_ref):
    o_ref[...] = r_ref[...] + _dot(a_ref[...], w_ref[...])


def _mm_res(a, w, res, *, tm=512, tn=2048):
    S, K = a.shape
    N = w.shape[1]
    return pl.pallas_call(
        _mm_res_kernel,
        grid=(S // tm, N // tn),
        in_specs=[
            pl.BlockSpec((tm, K), lambda i, j: (i, 0)),
            pl.BlockSpec((K, tn), lambda i, j: (0, j)),
            pl.BlockSpec((tm, tn), lambda i, j: (i, j)),
        ],
        out_specs=pl.BlockSpec((tm, tn), lambda i, j: (i, j)),
        out_shape=jax.ShapeDtypeStruct((S, N), F32),
        compiler_params=_cparams("arbitrary", "arbitrary"),
        name="mm_res",
    )(a, w, res)


def _ffn_kernel(x_ref, nw_ref, wa_ref, wu_ref, cwa_ref, cwu_ref, cba_ref, cbu_ref, wd_ref,
                o_ref, xn_scr, halo_a, halo_u):
    i = pl.program_id(0)
    j = pl.program_id(1)

    @pl.when(j == 0)
    def _():
        x = x_ref[...]
        xn_scr[...] = _rmsnorm_rows(x, nw_ref[...]).astype(BF16)
        o_ref[...] = x

    @pl.when(i == 0)
    def _():
        halo_a[j] = jnp.zeros(halo_a.shape[1:], F32)
        halo_u[j] = jnp.zeros(halo_u.shape[1:], F32)

    xn = xn_scr[...]
    tm = xn.shape[0]
    tf = wa_ref.shape[1]
    row = lax.broadcasted_iota(jnp.int32, (tm, tf), 0)
    is_row0 = row == 0
    is_row1 = row == 1

    def conv(h, halo_ref, cw_ref, cb_ref):
        halo = halo_ref[j]
        prev1 = halo[HALO_ROWS - 1:HALO_ROWS, :]
        prev2 = halo[HALO_ROWS - 2:HALO_ROWS - 1, :]
        h1 = jnp.where(is_row0, prev1, pltpu.roll(h, 1, 0))
        h2 = jnp.where(is_row0, prev2, jnp.where(is_row1, prev1, pltpu.roll(h, 2, 0)))
        halo_ref[j] = h[tm - HALO_ROWS:tm, :]
        return cb_ref[...] + cw_ref[0:1, :] * h2 + cw_ref[1:2, :] * h1 + cw_ref[2:3, :] * h

    ca = conv(_dot(xn, wa_ref[...]), halo_a, cwa_ref, cba_ref)
    cu = conv(_dot(xn, wu_ref[...]), halo_u, cwu_ref, cbu_ref)
    act = (ca * (1.0 / (1.0 + jnp.exp(-ca)))) * cu
    o_ref[...] += _dot(act.astype(BF16), wd_ref[...])


def _conv_ffn(x, norm_w, w_up, conv_w, conv_b, w_down, *, tm=512, tf=512):
    S, D = x.shape
    Fh = w_down.shape[0]
    nF = Fh // tf
    return pl.pallas_call(
        _ffn_kernel,
        grid=(S // tm, nF),
        in_specs=[
            pl.BlockSpec((tm, D), lambda i, j: (i, 0)),
            pl.BlockSpec((1, D), lambda i, j: (0, 0)),
            pl.BlockSpec((D, tf), lambda i, j: (0, j)),
            pl.BlockSpec((D, tf), lambda i, j: (0, nF + j)),
            pl.BlockSpec((CONV_WIDTH, tf), lambda i, j: (0, j)),
            pl.BlockSpec((CONV_WIDTH, tf), lambda i, j: (0, nF + j)),
            pl.BlockSpec((1, tf), lambda i, j: (0, j)),
            pl.BlockSpec((1, tf), lambda i, j: (0, nF + j)),
            pl.BlockSpec((tf, D), lambda i, j: (j, 0)),
        ],
        out_specs=pl.BlockSpec((tm, D), lambda i, j: (i, 0)),
        out_shape=jax.ShapeDtypeStruct((S, D), F32),
        scratch_shapes=[pltpu.VMEM((tm, D), BF16),
                        pltpu.VMEM((nF, HALO_ROWS, tf), F32),
                        pltpu.VMEM((nF, HALO_ROWS, tf), F32)],
        compiler_params=_cparams("arbitrary", "arbitrary"),
        name="conv_ffn",
    )(x, norm_w, w_up, w_up, conv_w, conv_w, conv_b, conv_b, w_down)


def _proj_headnorm_kernel(x_ref, nw_ref, w_ref, hw_ref, o_ref, xn_scr, *, scale):
    @pl.when(pl.program_id(1) == 0)
    def _():
        xn_scr[...] = _rmsnorm_rows(x_ref[...], nw_ref[...]).astype(BF16)

    y = _dot(xn_scr[...], w_ref[...])
    Dh = hw_ref.shape[1]
    for hh in range(y.shape[1] // Dh):
        cols = slice(hh * Dh, (hh + 1) * Dh)
        yn = _rmsnorm_rows(y[:, cols], hw_ref[...])
        if scale is not None:
            yn = yn * scale
        o_ref[:, cols] = yn.astype(BF16)


def _proj_headnorm(x, norm_w, w, head_w, n_out, *, scale=None, tm=512, tn=2048):
    S, D = x.shape
    Dh = head_w.shape[1]
    return pl.pallas_call(
        functools.partial(_proj_headnorm_kernel, scale=scale),
        grid=(S // tm, n_out // tn),
        in_specs=[
            pl.BlockSpec((tm, D), lambda i, j: (i, 0)),
            pl.BlockSpec((1, D), lambda i, j: (0, 0)),
            pl.BlockSpec((D, tn), lambda i, j: (0, j)),
            pl.BlockSpec((1, Dh), lambda i, j: (0, 0)),
        ],
        out_specs=pl.BlockSpec((tm, tn), lambda i, j: (i, j)),
        out_shape=jax.ShapeDtypeStruct((S, n_out), BF16),
        scratch_shapes=[pltpu.VMEM((tm, D), BF16)],
        compiler_params=_cparams("arbitrary", "arbitrary"),
        name="proj_headnorm",
    )(x, norm_w, w, head_w)


def _vproj_kernel(x_ref, nw_ref, w_ref, vt_ref, xn_scr):
    @pl.when(pl.program_id(1) == 0)
    def _():
        xn_scr[...] = _rmsnorm_rows(x_ref[...], nw_ref[...]).astype(BF16)

    y = _dot(xn_scr[...], w_ref[...])
    n_h, n_b, rows, BLK = vt_ref.shape
    Dh = MOBA_HEAD_DIM
    ones = jnp.ones((rows - Dh, BLK), BF16)
    for hh in range(n_h):
        for bb in range(n_b):
            blk = y[bb * BLK:(bb + 1) * BLK, hh * Dh:(hh + 1) * Dh]
            vt_ref[hh, bb, 0:Dh, :] = blk.T.astype(BF16)
            vt_ref[hh, bb, Dh:rows, :] = ones


def _vproj(x, norm_w, w_kv, col0, *, tm=512, tn=2048):
    S, D = x.shape
    H, Dh, BLK = MOBA_HEADS, MOBA_HEAD_DIM, MOBA_BLOCK
    off = col0 // tn
    return pl.pallas_call(
        _vproj_kernel,
        grid=(S // tm, (H * Dh) // tn),
        in_specs=[
            pl.BlockSpec((tm, D), lambda i, j: (i, 0)),
            pl.BlockSpec((1, D), lambda i, j: (0, 0)),
            pl.BlockSpec((D, tn), lambda i, j: (0, off + j)),
        ],
        out_specs=pl.BlockSpec((tn // Dh, tm // BLK, VT_ROWS, BLK), lambda i, j: (j, i, 0, 0)),
        out_shape=jax.ShapeDtypeStruct((H, S // BLK, VT_ROWS, BLK), BF16),
        scratch_shapes=[pltpu.VMEM((tm, D), BF16)],
        compiler_params=_cparams("arbitrary", "arbitrary"),
        name="v_proj",
    )(x, norm_w, w_kv)


def _kmean_kernel(k_ref, o_ref):
    rows, cols = k_ref.shape
    k = k_ref[...].astype(F32).reshape(rows // MOBA_BLOCK, MOBA_BLOCK, cols)
    o_ref[...] = jnp.sum(k, axis=1) * (1.0 / MOBA_BLOCK)


def _kmean(k, *, tr=2048, tc=512):
    S, N = k.shape
    return pl.pallas_call(
        _kmean_kernel,
        grid=(S // tr, N // tc),
        in_specs=[pl.BlockSpec((tr, tc), lambda i, j: (i, j))],
        out_specs=pl.BlockSpec((tr // MOBA_BLOCK, tc), lambda i, j: (i, j)),
        out_shape=jax.ShapeDtypeStruct((S // MOBA_BLOCK, N), F32),
        compiler_params=_cparams("arbitrary", "arbitrary"),
        name="kmean",
    )(k)


def _gate_kernel(q_ref, km_ref, mask_ref):
    nb = km_ref.shape[0]
    tq = q_ref.shape[0]
    g = _dot_nt(km_ref[...].astype(BF16), q_ref[...])
    blk = lax.broadcasted_iota(jnp.int32, (nb, tq), 0)
    pos = pl.program_id(1) * tq + lax.broadcasted_iota(jnp.int32, (nb, tq), 1)
    own = lax.shift_right_logical(pos, int(math.log2(MOBA_BLOCK)))
    neg = -jnp.inf
    v = jnp.where(blk < own, g, neg)
    sel = jnp.zeros((nb, tq), jnp.bool_)
    for r in range(min(MOBA_TOPK, nb)):
        m = jnp.max(v, axis=0, keepdims=True)
        first = jnp.min(jnp.where(v == m, blk, nb), axis=0, keepdims=True)
        onehot = blk == first
        sel = sel | (onehot & (own > r))
        v = jnp.where(onehot, neg, v)
    mask_ref[0] = jnp.where(sel, 0.0, neg)


def _gate_mask(q, kmean, *, tq=2048):
    S = q.shape[0]
    H, Dh = MOBA_HEADS, MOBA_HEAD_DIM
    nb = kmean.shape[0]
    return pl.pallas_call(
        _gate_kernel,
        grid=(H, S // tq),
        in_specs=[
            pl.BlockSpec((tq, Dh), lambda h, t: (t, h)),
            pl.BlockSpec((nb, Dh), lambda h, t: (0, h)),
        ],
        out_specs=pl.BlockSpec((1, nb, tq), lambda h, t: (h, 0, t)),
        out_shape=jax.ShapeDtypeStruct((H, nb, S), F32),
        compiler_params=_cparams("arbitrary", "arbitrary"),
        name="gate_topk",
    )(q, kmean)


def _rel_bias_tile(tbl_ref, h, dist):
    max_exact = REL_BUCKETS // 2
    n = jnp.maximum(dist, 0)
    large = max_exact + (jnp.log(jnp.maximum(n, 1).astype(F32) / max_exact)
                         / math.log(REL_MAX_DISTANCE / max_exact)
                         * (REL_BUCKETS - max_exact)).astype(jnp.int32)
    large = jnp.minimum(large, REL_BUCKETS - 1)
    bucket = jnp.where(n < max_exact, n, large)
    out = jnp.zeros(dist.shape, F32)
    for b in range(REL_BUCKETS):
        out = jnp.where(bucket == b, tbl_ref[h, b], out)
    return out


def _moba_kernel(tbl_ref, q_ref, k_ref, vt_ref, mask_ref, o_ref,
                 bown_ref, bprev_ref, *score_bufs):
    BLK, G, Dh = MOBA_BLOCK, MOBA_KV_GROUP, MOBA_HEAD_DIM
    NH, nb = vt_ref.shape[0], vt_ref.shape[1]
    NBUF = len(score_bufs)
    PHASES = NBUF * MOBA_ROTATIONS
    STEP = PHASES * G
    head0 = pl.program_id(0) * NH
    i = pl.program_id(1)
    neg = -jnp.inf

    @pl.when(i == 0)
    def _():
        key = lax.broadcasted_iota(jnp.int32, (BLK, BLK), 0)
        qry = lax.broadcasted_iota(jnp.int32, (BLK, BLK), 1)
        dist = qry - key
        for hh in range(NH):
            own = _rel_bias_tile(tbl_ref, head0 + hh, dist) * LOG2E
            bown_ref[hh] = jnp.where(dist >= 0, own, neg)
            bprev_ref[hh] = _rel_bias_tile(tbl_ref, head0 + hh, dist + BLK) * LOG2E

    q = [q_ref[:, hh * Dh:(hh + 1) * Dh] for hh in range(NH)]

    def k_rows(hh, j, n):
        return k_ref[pl.ds(pl.multiple_of(j * BLK, BLK), n * BLK), hh * Dh:(hh + 1) * Dh]

    def colmax(s):
        return jnp.max(s, axis=0, keepdims=True)

    def put_scores(s_ref, j0):
        j0 = jnp.minimum(j0, nb - G)
        for hh in range(NH):
            s_ref[hh] = _dot_nt(k_rows(hh, j0, G), q[hh])

    for b in range(NBUF - 1):
        put_scores(score_bufs[b], b * G)

    jp = jnp.maximum(i - 1, 0)
    carry = []
    for hh in range(NH):
        s_own = _dot_nt(k_rows(hh, i, 1), q[hh]) + bown_ref[hh]
        s_prev = _dot_nt(k_rows(hh, jp, 1), q[hh]) + bprev_ref[hh]
        sel_prev = mask_ref[hh, pl.ds(jp, 1), :]
        m = jnp.maximum(colmax(s_own), colmax(s_prev) + sel_prev)
        p_own = jnp.exp2(s_own - m).astype(BF16)
        p_prev = jnp.exp2(s_prev - (m - sel_prev)).astype(BF16)
        carry += [m, _dot(vt_ref[hh, i], p_own) + _dot(vt_ref[hh, jp], p_prev)]

    far_bias = [tbl_ref[head0 + hh, REL_BUCKETS - 1] * LOG2E for hh in range(NH)]

    def update(s_ref, j0, carry):
        out = []
        for hh in range(NH):
            m, acc = carry[2 * hh], carry[2 * hh + 1]
            sel = []
            m_new = m
            for g in range(G):
                jl = jnp.minimum(j0 + g, nb - 1)
                row = (mask_ref[hh, pl.ds(jl, 1), :]
                       + jnp.where(j0 + g < i - 1, far_bias[hh], neg))
                sel.append(row)
                m_new = jnp.maximum(m_new, colmax(s_ref[hh, g * BLK:(g + 1) * BLK, :]) + row)
            acc = acc * jnp.exp2(m - m_new)
            for g in range(G):
                p = jnp.exp2(s_ref[hh, g * BLK:(g + 1) * BLK, :] - (m_new - sel[g])).astype(BF16)
                acc = acc + _dot(vt_ref[hh, jnp.minimum(j0 + g, nb - 1)], p)
            out += [m_new, acc]
        return out

    def trip(t, carry):
        j0 = t * STEP
        carry = list(carry)
        for ph in range(PHASES):
            put_scores(score_bufs[(ph + NBUF - 1) % NBUF], j0 + (ph + NBUF - 1) * G)
            carry = update(score_bufs[ph % NBUF], j0 + ph * G, carry)
        return tuple(carry)

    n_trips = (i + (STEP - 2)) // STEP
    carry = lax.fori_loop(0, n_trips, trip, tuple(carry))
    for hh in range(NH):
        acc = carry[2 * hh + 1]
        o_ref[:, hh * Dh:(hh + 1) * Dh] = (acc[:Dh] / acc[Dh:Dh + 1]).T.astype(BF16)


def _moba_attn(tbl, q, k, vt, mask):
    S = q.shape[0]
    H, Dh, BLK = MOBA_HEADS, MOBA_HEAD_DIM, MOBA_BLOCK
    nb = S // BLK
    NH, G = MOBA_HEADS_PER_STEP, MOBA_KV_GROUP
    return pl.pallas_call(
        _moba_kernel,
        grid=(H // NH, nb),
        in_specs=[
            pl.BlockSpec(memory_space=pltpu.SMEM),
            pl.BlockSpec((BLK, NH * Dh), lambda h, i: (i, h)),
            pl.BlockSpec((S, NH * Dh), lambda h, i: (0, h)),
            pl.BlockSpec((NH, nb, VT_ROWS, BLK), lambda h, i: (h, 0, 0, 0)),
            pl.BlockSpec((NH, nb, BLK), lambda h, i: (h, 0, i)),
        ],
        out_specs=pl.BlockSpec((BLK, NH * Dh), lambda h, i: (i, h)),
        out_shape=jax.ShapeDtypeStruct((S, H * Dh), BF16),
        scratch_shapes=([pltpu.VMEM((NH, BLK, BLK), F32), pltpu.VMEM((NH, BLK, BLK), F32)]
                        + [pltpu.VMEM((NH, G * BLK, BLK), F32)] * MOBA_SCORE_BUFFERS),
        compiler_params=_cparams("arbitrary", "arbitrary"),
        name="moba_attn",
    )(tbl, q, k, vt, mask)


def _row(v):
    return v.reshape(1, -1)


def kernel(x, gla_norm, gla_w_in, gla_gk_w1, gla_gk_w2, gla_gk_b, gla_o_norm, gla_w_out,
           kv_norm, kv_w, k_norm_w, moba_norm, moba_w_q, moba_q_norm, moba_w_out, rel_bias,
           ffn_norm, ffn_w_up, ffn_conv_w, ffn_conv_b, ffn_w_down):
    B, S, D = x.shape
    assert B == 1
    depth = ffn_norm.shape[0]
    n_a = gla_norm.shape[0]
    h = x[0]
    HD = MOBA_HEADS * MOBA_HEAD_DIM
    k_bf = vt = kmean = None
    tbl = rel_bias.T

    for layer in range(depth):
        if layer < n_a:
            a = layer
            R = gla_gk_w1.shape[2]
            w1p = jnp.pad(gla_gk_w1[a].astype(BF16), ((0, 0), (0, LANE - R)))
            w2p = jnp.pad(gla_gk_w2[a].astype(BF16), ((0, LANE - R), (0, 0)))
            proj, log_a = _gla_in(h, _row(gla_norm[a]), gla_w_in[a].astype(BF16), w1p, w2p,
                                  _row(gla_gk_b[a]))
            o = _gla_core(proj, log_a, _row(gla_o_norm[a]))
            h = _mm_res(o, gla_w_out[a].astype(BF16), h)
        else:
            b = layer - n_a
            if k_bf is None:
                kv_bf = kv_w.astype(BF16)
                k_bf = _proj_headnorm(h, _row(kv_norm), kv_bf, _row(k_norm_w), HD)
                vt = _vproj(h, _row(kv_norm), kv_bf, HD)
                kmean = _kmean(k_bf)
            q = _proj_headnorm(h, _row(moba_norm[b]), moba_w_q[b].astype(BF16),
                               _row(moba_q_norm[b]), HD,
                               scale=MOBA_HEAD_DIM ** -0.5 * LOG2E)
            mask = _gate_mask(q, kmean)
            o = _moba_attn(tbl, q, k_bf, vt, mask)
            h = _mm_res(o, moba_w_out[b].astype(BF16), h)
        h = _conv_ffn(h, _row(ffn_norm[layer]), ffn_w_up[layer].astype(BF16),
                      ffn_conv_w[layer], _row(ffn_conv_b[layer]), ffn_w_down[layer].astype(BF16))
    return h[None]
```
